```python
import math
import jax
import jax.numpy as jnp
from jax import lax
import numpy as np

D_MODEL = 2048
BATCH = 2
SEQ = 4096
DEPTH = 2
DEC_BATCH = 32
DEC_SEQ = 4
PAST_LEN = 8192
PAGE_SIZE = 128

N_BRANCH = 4
W_BR = D_MODEL // 4
H_A = 4
DK_A = W_BR // (2 * H_A)
DV_A = W_BR // H_A
GATE_RANK = 16
GATE_TAU = 16.0
GLA_CHUNK = 16
H_B = 4
DH_B = W_BR // (2 * H_B)
H_C = 4
DH_C = W_BR // H_C
CONV_W = 4
MLSTM_CHUNK = 64
H_D = 4
DH_D = W_BR // H_D
H_I = 4
D_I = 64
TOPK_MAX = 256
Q_BLOCK = 128
ROPE_THETA = 10000.0
LN_EPS = 1e-5
NORM_EPS = 1e-6
DN_ALPHA = (2.0 * DEPTH) ** 0.25
DN_BETA = (8.0 * DEPTH) ** -0.25

IN_WIDTHS = (
    ('a_q', H_A * DK_A), ('a_k', H_A * DK_A), ('a_v', W_BR), ('a_g', GATE_RANK), ('a_z', W_BR),
    ('b_q', W_BR), ('b_k', W_BR), ('b_v', W_BR), ('b_z', W_BR),
    ('c_q', W_BR), ('c_k', W_BR), ('c_v', W_BR), ('c_i', H_C), ('c_f', H_C), ('c_o', W_BR), ('c_z', W_BR),
    ('d_q', W_BR), ('d_k', W_BR), ('d_v', W_BR), ('d_qi', H_I * D_I), ('d_ki', D_I), ('d_w', H_I), ('d_z', W_BR),
    ('gate', N_BRANCH * D_MODEL),
)
N_IN = sum(w for _, w in IN_WIDTHS)

kernel_name = 'hybrid_gla_diff_mlstm_dsa_step'


def _split_in(u):
    parts = {}
    off = 0
    for name, width in IN_WIDTHS:
        parts[name] = u[..., off:off + width]
        off += width
    return parts


def _rope(x, pos):
    half = x.shape[-1] // 2
    inv = ROPE_THETA ** (-jnp.arange(half, dtype=jnp.float32) / half)
    ang = pos.astype(jnp.float32)[:, None] * inv[None, :]
    cos = jnp.cos(ang)[:, None, :]
    sin = jnp.sin(ang)[:, None, :]
    x1 = x[..., :half].astype(jnp.float32)
    x2 = x[..., half:].astype(jnp.float32)
    return jnp.concatenate([x1 * cos - x2 * sin, x1 * sin + x2 * cos], -1).astype(x.dtype)


def _head_rms(h, g):
    hf = h.astype(jnp.float32)
    hf = hf * lax.rsqrt(jnp.mean(hf * hf, -1, keepdims=True) + NORM_EPS)
    return (hf.reshape(h.shape[:-2] + (-1,)) * g).astype(h.dtype)


def _layernorm(x, g, b):
    xf = x.astype(jnp.float32)
    xc = xf - jnp.mean(xf, -1, keepdims=True)
    var = jnp.mean(xc * xc, -1, keepdims=True)
    return (xc * lax.rsqrt(var + LN_EPS) * g + b).astype(x.dtype)


def _causal_conv(xs, buf, w):
    t = xs.shape[1]
    full = jnp.concatenate([buf.astype(xs.dtype), xs], 1)
    out = full[:, 0:t] * w[0]
    for j in range(1, CONV_W):
        out = out + full[:, j:j + t] * w[j]
    return out, full[:, t:]


def _gla(q, k, v, log_a, s0):
    b, t, h, dk = q.shape
    dt = v.dtype
    c = math.gcd(t, GLA_CHUNK)
    n = t // c
    f32 = jnp.float32
    def chunks(z):
        return z.astype(f32).reshape(b, n, c, h, -1).transpose(1, 0, 3, 2, 4)
    mask = jnp.tril(jnp.ones((c, c), bool))
    def step(s, inp):
        qb, kb, vb, ab = inp
        cb = jnp.cumsum(ab, axis=2)
        o_inter = jnp.einsum('bhtk,bhkv->bhtv', qb * jnp.exp(cb), s)
        diff = cb[:, :, :, None, :] - cb[:, :, None, :, :]
        decay = jnp.exp(jnp.where(mask[:, :, None], diff, -jnp.inf))
        att = jnp.einsum('bhtk,bhsk,bhtsk->bhts', qb, kb, decay)
        o = o_inter + att @ vb
        c_last = cb[:, :, -1:, :]
        s_new = jnp.exp(c_last[:, :, 0, :])[..., None] * s + jnp.einsum('bhsk,bhsv->bhkv', kb * jnp.exp(c_last - cb), vb)
        return s_new, o
    s_fin, o = lax.scan(step, s0.astype(f32), (chunks(q * (dk ** -0.5)), chunks(k), chunks(v), chunks(log_a)))
    o = o.transpose(1, 0, 3, 2, 4).reshape(b, t, h, -1)
    return o.astype(dt), s_fin.astype(dt)


def _mlstm(q, k, v, i_pre, f_pre, c0, n0, m0):
    b, t, h, d = q.shape
    dt = v.dtype
    f32 = jnp.float32
    c = math.gcd(t, MLSTM_CHUNK)
    n = t // c
    def chunks(z):
        return z.astype(f32).reshape(b, n, c, h, -1).transpose(1, 0, 3, 2, 4)
    def gchunks(z):
        return z.astype(f32).reshape(b, n, c, h).transpose(1, 0, 3, 2)
    mask = jnp.tril(jnp.ones((c, c), bool))
    def step(carry, inp):
        cm, nv, m = carry
        qb, kb, vb, ib, fb = inp
        fcum = jnp.cumsum(jax.nn.log_sigmoid(fb), -1)
        dmat = jnp.where(mask, fcum[..., :, None] - fcum[..., None, :] + ib[..., None, :], -jnp.inf)
        inter = fcum + m[..., None]
        m_t = jnp.maximum(inter, jnp.max(dmat, -1))
        w_inter = jnp.exp(inter - m_t)
        qk = jnp.einsum('bhtd,bhsd->bhts', qb, kb) * jnp.exp(dmat - m_t[..., None])
        num = w_inter[..., None] * jnp.einsum('bhtd,bhde->bhte', qb, cm) + qk @ vb
        den = w_inter * jnp.einsum('bhtd,bhd->bht', qb, nv) + jnp.sum(qk, -1)
        hv = num / jnp.maximum(jnp.abs(den), jnp.exp(-m_t))[..., None]
        f_last = fcum[..., -1]
        g_s = f_last[..., None] - fcum + ib
        m_new = jnp.maximum(f_last + m, jnp.max(g_s, -1))
        a = jnp.exp(f_last + m - m_new)
        ws = jnp.exp(g_s - m_new[..., None])
        c_new = a[..., None, None] * cm + jnp.einsum('bhs,bhsd,bhse->bhde', ws, kb, vb)
        n_new = a[..., None] * nv + jnp.einsum('bhs,bhsd->bhd', ws, kb)
        return (c_new, n_new, m_new), hv
    (c_f, n_f, m_f), hs = lax.scan(step, (c0.astype(f32), n0.astype(f32), m0.astype(f32)),
                                   (chunks(q), chunks(k), chunks(v), gchunks(i_pre), gchunks(f_pre)))
    hs = hs.transpose(1, 0, 3, 2, 4).reshape(b, t, h, d)
    return hs.astype(dt), c_f.astype(dt), n_f.astype(dt), m_f.astype(dt)


def _diff_attn_prompt(q, k, v, lam):
    b, s, h, _, d = q.shape
    nb = s // Q_BLOCK
    qb = q.reshape(b, nb, Q_BLOCK, h, 2, d).swapaxes(0, 1)
    kpos = jnp.arange(s)
    def one_block(args):
        qi, i = args
        qpos = i * Q_BLOCK + jnp.arange(Q_BLOCK)
        sc = jnp.einsum('bthmd,bshmd->bhmts', qi, k).astype(jnp.float32) * (d ** -0.5)
        sc = jnp.where(kpos[None, :] <= qpos[:, None], sc, -jnp.inf)
        p = jax.nn.softmax(sc, -1)
        a = (p[:, :, 0] - lam * p[:, :, 1]).astype(v.dtype)
        return jnp.einsum('bhts,bshe->bthe', a, v)
    o = lax.map(one_block, (qb, jnp.arange(nb)))
    return o.swapaxes(0, 1).reshape(b, s, h, 2 * d)


def _diff_attn_sample(q, k, v, k_past, v_past, lam):
    t = q.shape[1]
    d = q.shape[-1]
    p_len = k_past.shape[1]
    causal = jnp.tril(jnp.ones((t, t), bool))
    s_p = jnp.einsum('bthmd,bshmd->bhmts', q, k_past).astype(jnp.float32)
    s_n = jnp.where(causal, jnp.einsum('bthmd,bshmd->bhmts', q, k).astype(jnp.float32), -jnp.inf)
    p = jax.nn.softmax(jnp.concatenate([s_p, s_n], -1) * (d ** -0.5), -1)
    a = (p[:, :, 0] - lam * p[:, :, 1]).astype(v.dtype)
    return jnp.einsum('bhts,bshe->bthe', a[..., :p_len], v_past) + jnp.einsum('bhts,bshe->bthe', a[..., p_len:], v)


def _index_scores(qi, ki, wi):
    sc = jnp.einsum('bthi,bsi->bths', qi, ki).astype(jnp.float32) * (D_I ** -0.5)
    return jnp.einsum('bths,bth->bts', jax.nn.relu(sc), wi.astype(jnp.float32))


def _dsa_prompt(q, k, v, qi, ki, wi):
    b, s, h, dh = q.shape
    n_sel = min(TOPK_MAX, s // 4)
    nb = s // Q_BLOCK
    def blk(z):
        return z.reshape((b, nb, Q_BLOCK) + z.shape[2:]).swapaxes(0, 1)
    kpos = jnp.arange(s)
    gather = jax.vmap(lambda rows, sel: rows[sel])
    def one_block(args):
        qb, qib, wib, i = args
        qpos = i * Q_BLOCK + jnp.arange(Q_BLOCK)
        score = jnp.where(kpos[None, :] <= qpos[:, None], _index_scores(qib, ki, wib), -jnp.inf)
        _, sel = lax.top_k(score, n_sel)
        kg = gather(k, sel)
        vg = gather(v, sel)
        att = jnp.einsum('bthd,btkhd->bhtk', qb, kg).astype(jnp.float32) * (dh ** -0.5)
        valid = sel <= qpos[None, :, None]
        p = jax.nn.softmax(jnp.where(valid[:, None], att, -jnp.inf), -1)
        return jnp.einsum('bhtk,btkhd->bthd', p.astype(v.dtype), vg)
    o = lax.map(one_block, (blk(q), blk(qi), blk(wi), jnp.arange(nb)))
    return o.swapaxes(0, 1).reshape(b, s, h, dh)


def _dsa_sample(q, k, v, qi, ki, wi, ki_past, pool_k, pool_v, page_table):
    b, t, h, dh = q.shape
    p_len = ki_past.shape[1]
    n_sel = min(TOPK_MAX, (p_len + t) // 4)
    causal = jnp.tril(jnp.ones((t, t), bool))
    score = jnp.concatenate([_index_scores(qi, ki_past, wi),
                             jnp.where(causal, _index_scores(qi, ki, wi), -jnp.inf)], -1)
    _, sel = lax.top_k(score, n_sel)
    ps = jnp.minimum(sel, p_len - 1)
    phys = jnp.take_along_axis(page_table, (ps // PAGE_SIZE).reshape(b, -1), axis=1).reshape(sel.shape)
    flat = phys * PAGE_SIZE + ps % PAGE_SIZE
    ns = jnp.clip(sel - p_len, 0, t - 1)
    gather = jax.vmap(lambda rows, idx: rows[idx])
    is_past = (sel < p_len)[..., None, None]
    kg = jnp.where(is_past, pool_k.reshape((-1, h, dh))[flat], gather(k, ns))
    vg = jnp.where(is_past, pool_v.reshape((-1, h, dh))[flat], gather(v, ns))
    att = jnp.einsum('bthd,btkhd->bhtk', q, kg).astype(jnp.float32) * (dh ** -0.5)
    valid = sel <= (p_len + jnp.arange(t))[None, :, None]
    p = jax.nn.softmax(jnp.where(valid[:, None], att, -jnp.inf), -1)
    return jnp.einsum('bhtk,btkhd->bthd', p.astype(v.dtype), vg)


def _sublayer(x, pos, l, w_in, w_a_gate, b_a_gate, g_a, lam_qk, g_b, b_c_if, w_c_conv, g_c,
              w_branch, w_out, rec, paged):
    b, t, _ = x.shape
    f32 = jnp.float32
    a_s0, c_c0, c_n0, c_m0, c_conv0 = rec
    u = _split_in(jnp.einsum('btd,dn->btn', x, w_in[l]))
    qa = u['a_q'].reshape(b, t, H_A, DK_A)
    ka = u['a_k'].reshape(b, t, H_A, DK_A)
    va = u['a_v'].reshape(b, t, H_A, DV_A)
    log_a = jax.nn.log_sigmoid((u['a_g'] @ w_a_gate[l] + b_a_gate[l]).astype(f32)) / GATE_TAU
    oa, a_s = _gla(qa, ka, va, log_a.reshape(b, t, H_A, DK_A), a_s0)
    ya = _head_rms(oa, g_a[l]) * jax.nn.silu(u['a_z'])
    qb = _rope(u['b_q'].reshape(b, t, 2 * H_B, DH_B), pos).reshape(b, t, H_B, 2, DH_B)
    kb = _rope(u['b_k'].reshape(b, t, 2 * H_B, DH_B), pos).reshape(b, t, H_B, 2, DH_B)
    vb = u['b_v'].reshape(b, t, H_B, 2 * DH_B)
    lam_init = 0.8 - 0.6 * math.exp(-0.3 * l)
    lq = lam_qk[l].astype(f32)
    lam = jnp.exp(jnp.sum(lq[0] * lq[1])) - jnp.exp(jnp.sum(lq[2] * lq[3])) + lam_init
    if paged is None:
        ob = _diff_attn_prompt(qb, kb, vb, lam)
    else:
        page_table, pb_k, pb_v, pd_k, pd_v, pd_ki = paged
        kp = pb_k[page_table].reshape(b, -1, H_B, 2, DH_B)
        vp = pb_v[page_table].reshape(b, -1, H_B, 2 * DH_B)
        ob = _diff_attn_sample(qb, kb, vb, kp, vp, lam)
    yb = _head_rms(ob, g_b[l]) * (1.0 - lam_init) * jax.nn.silu(u['b_z'])
    qk_pre = jnp.concatenate([u['c_q'], u['c_k']], -1)
    qk_conv, c_conv = _causal_conv(qk_pre, c_conv0, w_c_conv[l])
    qk_conv = jax.nn.silu(qk_conv)
    qc = qk_conv[..., :W_BR].reshape(b, t, H_C, DH_C)
    kc = qk_conv[..., W_BR:].reshape(b, t, H_C, DH_C) * (DH_C ** -0.5)
    vc = u['c_v'].reshape(b, t, H_C, DH_C)
    i_pre = u['c_i'].astype(f32) + b_c_if[l, 0]
    f_pre = u['c_f'].astype(f32) + b_c_if[l, 1]
    hc, c_c, c_n, c_m = _mlstm(qc, kc, vc, i_pre, f_pre, c_c0, c_n0, c_m0)
    hc = jax.nn.sigmoid(u['c_o']).reshape(b, t, H_C, DH_C) * hc
    yc = _head_rms(hc, g_c[l]) * jax.nn.silu(u['c_z'])
    qd = _rope(u['d_q'].reshape(b, t, H_D, DH_D), pos)
    kd = _rope(u['d_k'].reshape(b, t, H_D, DH_D), pos)
    vd = u['d_v'].reshape(b, t, H_D, DH_D)
    qi = _rope(u['d_qi'].reshape(b, t, H_I, D_I), pos)
    ki = _rope(u['d_ki'].reshape(b, t, 1, D_I), pos)[:, :, 0]
    wi = u['d_w'] * (H_I ** -0.5)
    if paged is None:
        od = _dsa_prompt(qd, kd, vd, qi, ki, wi)
    else:
        ki_p = pd_ki[page_table].reshape(b, -1, D_I)
        od = _dsa_sample(qd, kd, vd, qi, ki, wi, ki_p, pd_k, pd_v, page_table)
    yd = od.reshape(b, t, W_BR) * jax.nn.silu(u['d_z'])
    ys = jnp.stack([ya, yb, yc, yd], 2)
    proj = jnp.einsum('btnw,nwd->btnd', ys, w_branch[l])
    gates = jax.nn.sigmoid(u['gate'].reshape(b, t, N_BRANCH, D_MODEL))
    out = jnp.einsum('btd,de->bte', jnp.sum(gates * proj, 2), w_out[l])
    return out, (kb, vb, kd, vd, ki, a_s, c_c, c_n, c_m, c_conv)


def setup_inputs(seed: int = 0) -> dict:
    key = jax.random.key(seed)
    ks = jax.random.split(key, 32)
    f32 = jnp.float32
    n_pages = PAST_LEN // PAGE_SIZE
    n_pool = (5 * DEC_BATCH * n_pages + 3) // 4
    def nrm(k, shape, s=1.0):
        return s * jax.random.normal(k, shape, f32)
    page_table = jax.random.permutation(ks[12], n_pool)[:DEC_BATCH * n_pages].reshape(DEC_BATCH, n_pages).astype(jnp.int32)
    b_i = nrm(ks[20], (DEPTH, H_C), 0.1)
    b_f = jnp.linspace(3.0, 6.0, H_C, dtype=f32)[None, :] + nrm(ks[21], (DEPTH, H_C), 0.01)
    return {
        'x_prompt': nrm(ks[0], (BATCH, SEQ, D_MODEL)),
        'x_sample': nrm(ks[1], (DEC_BATCH, DEC_SEQ, D_MODEL)),
        'cache_b_k': nrm(ks[2], (DEPTH, n_pool, PAGE_SIZE, H_B, 2, DH_B)),
        'cache_b_v': nrm(ks[3], (DEPTH, n_pool, PAGE_SIZE, H_B, 2 * DH_B)),
        'cache_d_k': nrm(ks[4], (DEPTH, n_pool, PAGE_SIZE, H_D, DH_D)),
        'cache_d_v': nrm(ks[5], (DEPTH, n_pool, PAGE_SIZE, H_D, DH_D)),
        'cache_d_ki': nrm(ks[6], (DEPTH, n_pool, PAGE_SIZE, D_I)),
        'state_a_s': nrm(ks[7], (DEPTH, DEC_BATCH, H_A, DK_A, DV_A)),
        'state_c_c': nrm(ks[8], (DEPTH, DEC_BATCH, H_C, DH_C, DH_C)),
        'state_c_n': nrm(ks[9], (DEPTH, DEC_BATCH, H_C, DH_C)),
        'state_c_m': nrm(ks[10], (DEPTH, DEC_BATCH, H_C), 0.5),
        'state_c_conv': nrm(ks[11], (DEPTH, DEC_BATCH, CONV_W - 1, 2 * W_BR)),
        'page_table': page_table,
        'w_in': nrm(ks[13], (DEPTH, D_MODEL, N_IN), D_MODEL ** -0.5),
        'w_a_gate': nrm(ks[14], (DEPTH, GATE_RANK, H_A * DK_A), GATE_RANK ** -0.5),
        'b_a_gate': nrm(ks[15], (DEPTH, H_A * DK_A), 0.02),
        'g_a': 1.0 + nrm(ks[16], (DEPTH, W_BR), 0.02),
        'lam_qk': nrm(ks[17], (DEPTH, 4, DH_B), 0.1),
        'g_b': 1.0 + nrm(ks[18], (DEPTH, W_BR), 0.02),
        'b_c_if': jnp.stack([b_i, b_f], 1),
        'w_c_conv': nrm(ks[22], (DEPTH, CONV_W, 2 * W_BR), CONV_W ** -0.5),
        'g_c': 1.0 + nrm(ks[23], (DEPTH, W_BR), 0.02),
        'w_branch': nrm(ks[24], (DEPTH, N_BRANCH, W_BR, D_MODEL), DN_BETA * W_BR ** -0.5),
        'w_out': nrm(ks[25], (DEPTH, D_MODEL, D_MODEL), DN_BETA * D_MODEL ** -0.5),
        'ln_g': 1.0 + nrm(ks[26], (DEPTH, D_MODEL), 0.02),
        'ln_b': nrm(ks[27], (DEPTH, D_MODEL), 0.02),
    }


def reference(x_prompt, x_sample, cache_b_k, cache_b_v, cache_d_k, cache_d_v, cache_d_ki,
              state_a_s, state_c_c, state_c_n, state_c_m, state_c_conv, page_table,
              w_in, w_a_gate, b_a_gate, g_a, lam_qk, g_b, b_c_if, w_c_conv, g_c,
              w_branch, w_out, ln_g, ln_b):
    bp, tp, _ = x_prompt.shape
    bd, td, _ = x_sample.shape
    past = page_table.shape[1] * PAGE_SIZE
    pos_p = jnp.arange(tp)
    pos_s = past + jnp.arange(td)
    dt = x_prompt.dtype
    hp = x_prompt
    hs = x_sample
    new_p = []
    new_s = []
    for l in range(DEPTH):
        rec_p = (jnp.zeros((bp, H_A, DK_A, DV_A), dt), jnp.zeros((bp, H_C, DH_C, DH_C), dt),
                 jnp.zeros((bp, H_C, DH_C), dt), jnp.zeros((bp, H_C), dt),
                 jnp.zeros((bp, CONV_W - 1, 2 * W_BR), dt))
        out_p, st_p = _sublayer(hp, pos_p, l, w_in, w_a_gate, b_a_gate, g_a, lam_qk, g_b, b_c_if,
                                w_c_conv, g_c, w_branch, w_out, rec_p, None)
        hp = _layernorm(DN_ALPHA * hp + out_p, ln_g[l], ln_b[l])
        rec_s = (state_a_s[l], state_c_c[l], state_c_n[l], state_c_m[l], state_c_conv[l])
        paged = (page_table, cache_b_k[l], cache_b_v[l], cache_d_k[l], cache_d_v[l], cache_d_ki[l])
        out_s, st_s = _sublayer(hs, pos_s, l, w_in, w_a_gate, b_a_gate, g_a, lam_qk, g_b, b_c_if,
                                w_c_conv, g_c, w_branch, w_out, rec_s, paged)
        hs = _layernorm(DN_ALPHA * hs + out_s, ln_g[l], ln_b[l])
        new_p.append(st_p)
        new_s.append(st_s)
    p_b_k, p_b_v, p_d_k, p_d_v, p_d_ki, p_a_s, p_c_c, p_c_n, p_c_m, p_c_conv = [
        jnp.stack([st[i] for st in new_p], 0) for i in range(10)]
    s_b_k, s_b_v, s_d_k, s_d_v, s_d_ki, s_a_s, s_c_c, s_c_n, s_c_m, s_c_conv = [
        jnp.stack([st[i] for st in new_s], 0) for i in range(10)]
    return (hp, hs, p_b_k, p_b_v, p_d_k, p_d_v, p_d_ki, p_a_s, p_c_c, p_c_n, p_c_m, p_c_conv,
            s_b_k, s_b_v, s_d_k, s_d_v, s_d_ki, s_a_s, s_c_c, s_c_n, s_c_m, s_c_conv)
```

```python
import functools
import math
import jax
import jax.numpy as jnp
from jax import lax
import numpy as np
from jax.experimental import pallas as pl
from jax.experimental.pallas import tpu as pltpu

D_MODEL = 2048
DEPTH = 2
PAGE_SIZE = 128
N_BRANCH = 4
W_BR = D_MODEL // 4
H_A = 4
DK_A = W_BR // (2 * H_A)
DV_A = W_BR // H_A
GATE_RANK = 16
GATE_TAU = 16.0
GLA_CHUNK = 16
H_B = 4
DH_B = W_BR // (2 * H_B)
H_C = 4
DH_C = W_BR // H_C
CONV_W = 4
MLSTM_CHUNK = 64
H_D = 4
DH_D = W_BR // H_D
H_I = 4
D_I = 64
TOPK_MAX = 256
Q_BLOCK = 128
ROPE_THETA = 10000.0
LN_EPS = 1e-5
NORM_EPS = 1e-6
DN_ALPHA = (2.0 * DEPTH) ** 0.25

IN_WIDTHS = (
    ('a_q', H_A * DK_A), ('a_k', H_A * DK_A), ('a_v', W_BR), ('a_g', GATE_RANK), ('a_z', W_BR),
    ('b_q', W_BR), ('b_k', W_BR), ('b_v', W_BR), ('b_z', W_BR),
    ('c_q', W_BR), ('c_k', W_BR), ('c_v', W_BR), ('c_i', H_C), ('c_f', H_C), ('c_o', W_BR), ('c_z', W_BR),
    ('d_q', W_BR), ('d_k', W_BR), ('d_v', W_BR), ('d_qi', H_I * D_I), ('d_ki', D_I), ('d_w', H_I), ('d_z', W_BR),
    ('gate', N_BRANCH * D_MODEL),
)
N_IN = sum(w for _, w in IN_WIDTHS)
N_IN_PAD = 16896
LANES = 128


def _mm_kernel(x_ref, w_ref, o_ref):
    o_ref[...] = jnp.dot(x_ref[...], w_ref[...], preferred_element_type=jnp.float32)


def _matmul(x, w, tm, tn):
    m, k = x.shape
    n = w.shape[1]
    return pl.pallas_call(
        _mm_kernel,
        grid=(n // tn, m // tm),
        in_specs=[pl.BlockSpec((tm, k), lambda j, i: (i, 0)),
                  pl.BlockSpec((k, tn), lambda j, i: (0, j))],
        out_specs=pl.BlockSpec((tm, tn), lambda j, i: (i, j)),
        out_shape=jax.ShapeDtypeStruct((m, n), jnp.float32),
        compiler_params=pltpu.CompilerParams(
            dimension_semantics=("arbitrary", "arbitrary"), vmem_limit_bytes=48 * 1024 * 1024),
        name="in_proj",
    )(x, w)


def _split_in(u):
    parts = {}
    off = 0
    for name, width in IN_WIDTHS:
        parts[name] = u[..., off:off + width]
        off += width
    return parts


def _rope(x, pos):
    half = x.shape[-1] // 2
    inv = ROPE_THETA ** (-jnp.arange(half, dtype=jnp.float32) / half)
    ang = pos.astype(jnp.float32)[:, None] * inv[None, :]
    cos = jnp.cos(ang)[:, None, :]
    sin = jnp.sin(ang)[:, None, :]
    x1 = x[..., :half].astype(jnp.float32)
    x2 = x[..., half:].astype(jnp.float32)
    return jnp.concatenate([x1 * cos - x2 * sin, x1 * sin + x2 * cos], -1).astype(x.dtype)


def _head_rms(h, g):
    hf = h.astype(jnp.float32)
    hf = hf * lax.rsqrt(jnp.mean(hf * hf, -1, keepdims=True) + NORM_EPS)
    return (hf.reshape(h.shape[:-2] + (-1,)) * g).astype(h.dtype)


def _layernorm(x, g, b):
    xf = x.astype(jnp.float32)
    xc = xf - jnp.mean(xf, -1, keepdims=True)
    var = jnp.mean(xc * xc, -1, keepdims=True)
    return (xc * lax.rsqrt(var + LN_EPS) * g + b).astype(x.dtype)


def _causal_conv(xs, buf, w):
    t = xs.shape[1]
    full = jnp.concatenate([buf.astype(xs.dtype), xs], 1)
    out = full[:, 0:t] * w[0]
    for j in range(1, CONV_W):
        out = out + full[:, j:j + t] * w[j]
    return out, full[:, t:]


def _gla(q, k, v, log_a, s0):
    b, t, h, dk = q.shape
    dt = v.dtype
    c = math.gcd(t, GLA_CHUNK)
    n = t // c
    f32 = jnp.float32
    def chunks(z):
        return z.astype(f32).reshape(b, n, c, h, -1).transpose(1, 0, 3, 2, 4)
    mask = jnp.tril(jnp.ones((c, c), bool))
    def step(s, inp):
        qb, kb, vb, ab = inp
        cb = jnp.cumsum(ab, axis=2)
        o_inter = jnp.einsum('bhtk,bhkv->bhtv', qb * jnp.exp(cb), s)
        diff = cb[:, :, :, None, :] - cb[:, :, None, :, :]
        decay = jnp.exp(jnp.where(mask[:, :, None], diff, -jnp.inf))
        att = jnp.einsum('bhtk,bhsk,bhtsk->bhts', qb, kb, decay)
        o = o_inter + att @ vb
        c_last = cb[:, :, -1:, :]
        s_new = jnp.exp(c_last[:, :, 0, :])[..., None] * s + jnp.einsum('bhsk,bhsv->bhkv', kb * jnp.exp(c_last - cb), vb)
        return s_new, o
    s_fin, o = lax.scan(step, s0.astype(f32), (chunks(q * (dk ** -0.5)), chunks(k), chunks(v), chunks(log_a)))
    o = o.transpose(1, 0, 3, 2, 4).reshape(b, t, h, -1)
    return o.astype(dt), s_fin.astype(dt)


def _mlstm(q, k, v, i_pre, f_pre, c0, n0, m0):
    b, t, h, d = q.shape
    dt = v.dtype
    f32 = jnp.float32
    c = math.gcd(t, MLSTM_CHUNK)
    n = t // c
    def chunks(z):
        return z.astype(f32).reshape(b, n, c, h, -1).transpose(1, 0, 3, 2, 4)
    def gchunks(z):
        return z.astype(f32).reshape(b, n, c, h).transpose(1, 0, 3, 2)
    mask = jnp.tril(jnp.ones((c, c), bool))
    def step(carry, inp):
        cm, nv, m = carry
        qb, kb, vb, ib, fb = inp
        fcum = jnp.cumsum(jax.nn.log_sigmoid(fb), -1)
        dmat = jnp.where(mask, fcum[..., :, None] - fcum[..., None, :] + ib[..., None, :], -jnp.inf)
        inter = fcum + m[..., None]
        m_t = jnp.maximum(inter, jnp.max(dmat, -1))
        w_inter = jnp.exp(inter - m_t)
        qk = jnp.einsum('bhtd,bhsd->bhts', qb, kb) * jnp.exp(dmat - m_t[..., None])
        num = w_inter[..., None] * jnp.einsum('bhtd,bhde->bhte', qb, cm) + qk @ vb
        den = w_inter * jnp.einsum('bhtd,bhd->bht', qb, nv) + jnp.sum(qk, -1)
        hv = num / jnp.maximum(jnp.abs(den), jnp.exp(-m_t))[..., None]
        f_last = fcum[..., -1]
        g_s = f_last[..., None] - fcum + ib
        m_new = jnp.maximum(f_last + m, jnp.max(g_s, -1))
        a = jnp.exp(f_last + m - m_new)
        ws = jnp.exp(g_s - m_new[..., None])
        c_new = a[..., None, None] * cm + jnp.einsum('bhs,bhsd,bhse->bhde', ws, kb, vb)
        n_new = a[..., None] * nv + jnp.einsum('bhs,bhsd->bhd', ws, kb)
        return (c_new, n_new, m_new), hv
    (c_f, n_f, m_f), hs = lax.scan(step, (c0.astype(f32), n0.astype(f32), m0.astype(f32)),
                                   (chunks(q), chunks(k), chunks(v), gchunks(i_pre), gchunks(f_pre)))
    hs = hs.transpose(1, 0, 3, 2, 4).reshape(b, t, h, d)
    return hs.astype(dt), c_f.astype(dt), n_f.astype(dt), m_f.astype(dt)


def _diff_attn_prompt(q, k, v, lam):
    b, s, h, _, d = q.shape
    nb = s // Q_BLOCK
    qb = q.reshape(b, nb, Q_BLOCK, h, 2, d).swapaxes(0, 1)
    kpos = jnp.arange(s)
    def one_block(args):
        qi, i = args
        qpos = i * Q_BLOCK + jnp.arange(Q_BLOCK)
        sc = jnp.einsum('bthmd,bshmd->bhmts', qi, k).astype(jnp.float32) * (d ** -0.5)
        sc = jnp.where(kpos[None, :] <= qpos[:, None], sc, -jnp.inf)
        p = jax.nn.softmax(sc, -1)
        a = (p[:, :, 0] - lam * p[:, :, 1]).astype(v.dtype)
        return jnp.einsum('bhts,bshe->bthe', a, v)
    o = lax.map(one_block, (qb, jnp.arange(nb)))
    return o.swapaxes(0, 1).reshape(b, s, h, 2 * d)


def _diff_attn_sample(q, k, v, k_past, v_past, lam):
    t = q.shape[1]
    d = q.shape[-1]
    p_len = k_past.shape[1]
    causal = jnp.tril(jnp.ones((t, t), bool))
    s_p = jnp.einsum('bthmd,bshmd->bhmts', q, k_past).astype(jnp.float32)
    s_n = jnp.where(causal, jnp.einsum('bthmd,bshmd->bhmts', q, k).astype(jnp.float32), -jnp.inf)
    p = jax.nn.softmax(jnp.concatenate([s_p, s_n], -1) * (d ** -0.5), -1)
    a = (p[:, :, 0] - lam * p[:, :, 1]).astype(v.dtype)
    return jnp.einsum('bhts,bshe->bthe', a[..., :p_len], v_past) + jnp.einsum('bhts,bshe->bthe', a[..., p_len:], v)


def _index_scores(qi, ki, wi):
    sc = jnp.einsum('bthi,bsi->bths', qi, ki).astype(jnp.float32) * (D_I ** -0.5)
    return jnp.einsum('bths,bth->bts', jax.nn.relu(sc), wi.astype(jnp.float32))


def _dsa_prompt(q, k, v, qi, ki, wi):
    b, s, h, dh = q.shape
    n_sel = min(TOPK_MAX, s // 4)
    nb = s // Q_BLOCK
    def blk(z):
        return z.reshape((b, nb, Q_BLOCK) + z.shape[2:]).swapaxes(0, 1)
    kpos = jnp.arange(s)
    gather = jax.vmap(lambda rows, sel: rows[sel])
    def one_block(args):
        qb, qib, wib, i = args
        qpos = i * Q_BLOCK + jnp.arange(Q_BLOCK)
        score = jnp.where(kpos[None, :] <= qpos[:, None], _index_scores(qib, ki, wib), -jnp.inf)
        _, sel = lax.top_k(score, n_sel)
        kg = gather(k, sel)
        vg = gather(v, sel)
        att = jnp.einsum('bthd,btkhd->bhtk', qb, kg).astype(jnp.float32) * (dh ** -0.5)
        valid = sel <= qpos[None, :, None]
        p = jax.nn.softmax(jnp.where(valid[:, None], att, -jnp.inf), -1)
        return jnp.einsum('bhtk,btkhd->bthd', p.astype(v.dtype), vg)
    o = lax.map(one_block, (blk(q), blk(qi), blk(wi), jnp.arange(nb)))
    return o.swapaxes(0, 1).reshape(b, s, h, dh)


def _dsa_sample(q, k, v, qi, ki, wi, ki_past, pool_k, pool_v, page_table):
    b, t, h, dh = q.shape
    p_len = ki_past.shape[1]
    n_sel = min(TOPK_MAX, (p_len + t) // 4)
    causal = jnp.tril(jnp.ones((t, t), bool))
    score = jnp.concatenate([_index_scores(qi, ki_past, wi),
                             jnp.where(causal, _index_scores(qi, ki, wi), -jnp.inf)], -1)
    _, sel = lax.top_k(score, n_sel)
    ps = jnp.minimum(sel, p_len - 1)
    phys = jnp.take_along_axis(page_table, (ps // PAGE_SIZE).reshape(b, -1), axis=1).reshape(sel.shape)
    flat = phys * PAGE_SIZE + ps % PAGE_SIZE
    ns = jnp.clip(sel - p_len, 0, t - 1)
    gather = jax.vmap(lambda rows, idx: rows[idx])
    is_past = (sel < p_len)[..., None, None]
    kg = jnp.where(is_past, pool_k.reshape((-1, h, dh))[flat], gather(k, ns))
    vg = jnp.where(is_past, pool_v.reshape((-1, h, dh))[flat], gather(v, ns))
    att = jnp.einsum('bthd,btkhd->bhtk', q, kg).astype(jnp.float32) * (dh ** -0.5)
    valid = sel <= (p_len + jnp.arange(t))[None, :, None]
    p = jax.nn.softmax(jnp.where(valid[:, None], att, -jnp.inf), -1)
    return jnp.einsum('bhtk,btkhd->bthd', p.astype(v.dtype), vg)


def _sublayer(x, pos, l, w_in_p, w_a_gate, b_a_gate, g_a, lam_qk, g_b, b_c_if, w_c_conv, g_c,
              w_branch, w_out, rec, paged):
    b, t, _ = x.shape
    f32 = jnp.float32
    a_s0, c_c0, c_n0, c_m0, c_conv0 = rec
    m = b * t
    u_full = _matmul(x.reshape(m, D_MODEL).astype(jnp.bfloat16), w_in_p[l], min(m, 512), 1536)
    u = _split_in(u_full[:, :N_IN].reshape(b, t, N_IN))
    qa = u['a_q'].reshape(b, t, H_A, DK_A)
    ka = u['a_k'].reshape(b, t, H_A, DK_A)
    va = u['a_v'].reshape(b, t, H_A, DV_A)
    log_a = jax.nn.log_sigmoid((u['a_g'] @ w_a_gate[l] + b_a_gate[l]).astype(f32)) / GATE_TAU
    oa, a_s = _gla(qa, ka, va, log_a.reshape(b, t, H_A, DK_A), a_s0)
    ya = _head_rms(oa, g_a[l]) * jax.nn.silu(u['a_z'])
    qb = _rope(u['b_q'].reshape(b, t, 2 * H_B, DH_B), pos).reshape(b, t, H_B, 2, DH_B)
    kb = _rope(u['b_k'].reshape(b, t, 2 * H_B, DH_B), pos).reshape(b, t, H_B, 2, DH_B)
    vb = u['b_v'].reshape(b, t, H_B, 2 * DH_B)
    lam_init = 0.8 - 0.6 * math.exp(-0.3 * l)
    lq = lam_qk[l].astype(f32)
    lam = jnp.exp(jnp.sum(lq[0] * lq[1])) - jnp.exp(jnp.sum(lq[2] * lq[3])) + lam_init
    if paged is None:
        ob = _diff_attn_prompt(qb, kb, vb, lam)
    else:
        page_table, pb_k, pb_v, pd_k, pd_v, pd_ki = paged
        kp = pb_k[page_table].reshape(b, -1, H_B, 2, DH_B)
        vp = pb_v[page_table].reshape(b, -1, H_B, 2 * DH_B)
        ob = _diff_attn_sample(qb, kb, vb, kp, vp, lam)
    yb = _head_rms(ob, g_b[l]) * (1.0 - lam_init) * jax.nn.silu(u['b_z'])
    qk_pre = jnp.concatenate([u['c_q'], u['c_k']], -1)
    qk_conv, c_conv = _causal_conv(qk_pre, c_conv0, w_c_conv[l])
    qk_conv = jax.nn.silu(qk_conv)
    qc = qk_conv[..., :W_BR].reshape(b, t, H_C, DH_C)
    kc = qk_conv[..., W_BR:].reshape(b, t, H_C, DH_C) * (DH_C ** -0.5)
    vc = u['c_v'].reshape(b, t, H_C, DH_C)
    i_pre = u['c_i'].astype(f32) + b_c_if[l, 0]
    f_pre = u['c_f'].astype(f32) + b_c_if[l, 1]
    hc, c_c, c_n, c_m = _mlstm(qc, kc, vc, i_pre, f_pre, c_c0, c_n0, c_m0)
    hc = jax.nn.sigmoid(u['c_o']).reshape(b, t, H_C, DH_C) * hc
    yc = _head_rms(hc, g_c[l]) * jax.nn.silu(u['c_z'])
    qd = _rope(u['d_q'].reshape(b, t, H_D, DH_D), pos)
    kd = _rope(u['d_k'].reshape(b, t, H_D, DH_D), pos)
    vd = u['d_v'].reshape(b, t, H_D, DH_D)
    qi = _rope(u['d_qi'].reshape(b, t, H_I, D_I), pos)
    ki = _rope(u['d_ki'].reshape(b, t, 1, D_I), pos)[:, :, 0]
    wi = u['d_w'] * (H_I ** -0.5)
    if paged is None:
        od = _dsa_prompt(qd, kd, vd, qi, ki, wi)
    else:
        ki_p = pd_ki[page_table].reshape(b, -1, D_I)
        od = _dsa_sample(qd, kd, vd, qi, ki, wi, ki_p, pd_k, pd_v, page_table)
    yd = od.reshape(b, t, W_BR) * jax.nn.silu(u['d_z'])
    ys = jnp.stack([ya, yb, yc, yd], 2)
    proj = jnp.einsum('btnw,nwd->btnd', ys, w_branch[l])
    gates = jax.nn.sigmoid(u['gate'].reshape(b, t, N_BRANCH, D_MODEL))
    out = jnp.einsum('btd,de->bte', jnp.sum(gates * proj, 2), w_out[l])
    return out, (kb, vb, kd, vd, ki, a_s, c_c, c_n, c_m, c_conv)


def kernel(x_prompt, x_sample, cache_b_k, cache_b_v, cache_d_k, cache_d_v, cache_d_ki,
           state_a_s, state_c_c, state_c_n, state_c_m, state_c_conv, page_table,
           w_in, w_a_gate, b_a_gate, g_a, lam_qk, g_b, b_c_if, w_c_conv, g_c,
           w_branch, w_out, ln_g, ln_b):
    bp, tp, _ = x_prompt.shape
    bd, td, _ = x_sample.shape
    past = page_table.shape[1] * PAGE_SIZE
    pos_p = jnp.arange(tp)
    pos_s = past + jnp.arange(td)
    dt = x_prompt.dtype
    w_in_p = jnp.pad(w_in.astype(jnp.bfloat16), ((0, 0), (0, 0), (0, N_IN_PAD - N_IN)))
    hp = x_prompt
    hs = x_sample
    new_p = []
    new_s = []
    for l in range(DEPTH):
        rec_p = (jnp.zeros((bp, H_A, DK_A, DV_A), dt), jnp.zeros((bp, H_C, DH_C, DH_C), dt),
                 jnp.zeros((bp, H_C, DH_C), dt), jnp.zeros((bp, H_C), dt),
                 jnp.zeros((bp, CONV_W - 1, 2 * W_BR), dt))
        out_p, st_p = _sublayer(hp, pos_p, l, w_in_p, w_a_gate, b_a_gate, g_a, lam_qk, g_b, b_c_if,
                                w_c_conv, g_c, w_branch, w_out, rec_p, None)
        hp = _layernorm(DN_ALPHA * hp + out_p, ln_g[l], ln_b[l])
        rec_s = (state_a_s[l], state_c_c[l], state_c_n[l], state_c_m[l], state_c_conv[l])
        paged = (page_table, cache_b_k[l], cache_b_v[l], cache_d_k[l], cache_d_v[l], cache_d_ki[l])
        out_s, st_s = _sublayer(hs, pos_s, l, w_in_p, w_a_gate, b_a_gate, g_a, lam_qk, g_b, b_c_if,
                                w_c_conv, g_c, w_branch, w_out, rec_s, paged)
        hs = _layernorm(DN_ALPHA * hs + out_s, ln_g[l], ln_b[l])
        new_p.append(st_p)
        new_s.append(st_s)
    outs_p = [jnp.stack([st[i] for st in new_p], 0) for i in range(10)]
    outs_s = [jnp.stack([st[i] for st in new_s], 0) for i in range(10)]
    return (hp, hs, *outs_p, *outs_s)
```

```python
import functools
import math
import jax
import jax.numpy as jnp
from jax import lax
import numpy as np
from jax.experimental import pallas as pl
from jax.experimental.pallas import tpu as pltpu

D_MODEL = 2048
DEPTH = 2
PAGE_SIZE = 128
N_BRANCH = 4
W_BR = D_MODEL // 4
H_A = 4
DK_A = W_BR // (2 * H_A)
DV_A = W_BR // H_A
GATE_RANK = 16
GATE_TAU = 16.0
GLA_CHUNK = 16
H_B = 4
DH_B = W_BR // (2 * H_B)
H_C = 4
DH_C = W_BR // H_C
CONV_W = 4
MLSTM_CHUNK = 64
H_D = 4
DH_D = W_BR // H_D
H_I = 4
D_I = 64
TOPK_MAX = 256
Q_BLOCK = 128
ROPE_THETA = 10000.0
LN_EPS = 1e-5
NORM_EPS = 1e-6
DN_ALPHA = (2.0 * DEPTH) ** 0.25

IN_WIDTHS = (
    ('a_q', H_A * DK_A), ('a_k', H_A * DK_A), ('a_v', W_BR), ('a_g', GATE_RANK), ('a_z', W_BR),
    ('b_q', W_BR), ('b_k', W_BR), ('b_v', W_BR), ('b_z', W_BR),
    ('c_q', W_BR), ('c_k', W_BR), ('c_v', W_BR), ('c_i', H_C), ('c_f', H_C), ('c_o', W_BR), ('c_z', W_BR),
    ('d_q', W_BR), ('d_k', W_BR), ('d_v', W_BR), ('d_qi', H_I * D_I), ('d_ki', D_I), ('d_w', H_I), ('d_z', W_BR),
    ('gate', N_BRANCH * D_MODEL),
)
N_IN = sum(w for _, w in IN_WIDTHS)
N_IN_PAD = 16896
LANES = 128


def _mm_kernel(x_ref, w_ref, o_ref):
    o_ref[...] = jnp.dot(x_ref[...], w_ref[...], preferred_element_type=jnp.float32)


def _matmul(x, w, tm, tn):
    m, k = x.shape
    n = w.shape[1]
    return pl.pallas_call(
        _mm_kernel,
        grid=(n // tn, m // tm),
        in_specs=[pl.BlockSpec((tm, k), lambda j, i: (i, 0)),
                  pl.BlockSpec((k, tn), lambda j, i: (0, j))],
        out_specs=pl.BlockSpec((tm, tn), lambda j, i: (i, j)),
        out_shape=jax.ShapeDtypeStruct((m, n), jnp.float32),
        compiler_params=pltpu.CompilerParams(
            dimension_semantics=("arbitrary", "arbitrary"), vmem_limit_bytes=48 * 1024 * 1024),
        name="in_proj",
    )(x, w)


NEG_BIG = -1e30
INT_MIN = -2 ** 31
_NT = (((1,), (1,)), ((), ()))
VMEM_LIMIT = 48 * 1024 * 1024


def _split3(x):
    hi = x.astype(jnp.bfloat16)
    lo = (x - hi.astype(jnp.float32)).astype(jnp.bfloat16)
    return hi, lo


def _flash_chunk(qh, kh, vh, bias, scale, carry):
    m, l, acc = carry
    s = lax.dot_general(qh, kh, _NT, preferred_element_type=jnp.float32) * scale + bias
    m_new = jnp.maximum(m, jnp.max(s, axis=1, keepdims=True))
    alpha = jnp.exp(m - m_new)
    p = jnp.exp(s - m_new)
    l = alpha * l + jnp.sum(p, axis=1, keepdims=True)
    acc = alpha * acc + jnp.dot(p.astype(jnp.bfloat16), vh, preferred_element_type=jnp.float32)
    return m_new, l, acc


def _flash_init(tq, dv):
    return (jnp.full((tq, 1), NEG_BIG, jnp.float32), jnp.zeros((tq, 1), jnp.float32),
            jnp.zeros((tq, dv), jnp.float32))


def _dsa_prompt_kernel(q_ref, k_ref, v_ref, qi_ref, ki_ref, wi_ref, tri_ref, o_ref, key_ref, bias_ref,
                       *, tq, kc, n_sel, scale):
    f32 = jnp.float32
    i = pl.program_id(1)
    nc = (i * tq + tq + kc - 1) // kc
    qpos = i * tq + lax.broadcasted_iota(jnp.int32, (tq, 1), 0)
    lane = lax.broadcasted_iota(jnp.int32, (1, kc), 1)
    wi = wi_ref[0] * (H_I ** -0.5 * D_I ** -0.5)
    wis = [jnp.broadcast_to(wi[:, h:h + 1], (tq, kc)) for h in range(H_I)]
    d3 = 3 * D_I

    def score_body(c, carry):
        off = pl.multiple_of(c * kc, kc)
        kik = ki_ref[0, pl.ds(off, kc), :]
        acc = jnp.zeros((tq, kc), f32)
        for h in range(H_I):
            s = lax.dot_general(qi_ref[0, :, h * d3:(h + 1) * d3], kik, _NT, preferred_element_type=f32)
            acc = acc + wis[h] * jnp.maximum(s, 0.0)
        acc = jnp.where(off + lane <= qpos, acc, -jnp.inf)
        bits = pltpu.bitcast(acc, jnp.int32)
        bits = jnp.where(bits == INT_MIN, 0, bits)
        key_ref[c] = jnp.where(bits < 0, bits ^ 0x7FFFFFFF, bits)
        return carry
    lax.fori_loop(0, nc, score_body, 0)

    def count(pred, ref_val):
        def body(c, part):
            hit = jnp.where(pred(key_ref[c], ref_val), 1.0, 0.0)
            for j in range(kc // LANES):
                part = part + hit[:, j * LANES:(j + 1) * LANES]
            return part
        part = lax.fori_loop(0, nc, body, jnp.zeros((tq, LANES), f32))
        return jnp.sum(part, axis=1, keepdims=True)

    def bit_body(it, tau):
        cand = tau ^ jnp.left_shift(jnp.int32(1), 31 - it)
        return jnp.where(count(lambda a, b: a >= b, cand) >= n_sel, cand, tau)
    tau = lax.fori_loop(0, 32, bit_body, jnp.full((tq, 1), INT_MIN, jnp.int32))
    need = n_sel - count(lambda a, b: a > b, tau)

    def bias_body(c, seen):
        key = key_ref[c]
        tie = key == tau
        tie_f = jnp.where(tie, 1.0, 0.0)
        before = seen + jnp.dot(tie_f.astype(jnp.bfloat16), tri_ref[...], preferred_element_type=f32)
        sel = ((key > tau) | (tie & (before < need))) & (c * kc + lane <= qpos)
        bias_ref[c] = jnp.where(sel, 0.0, NEG_BIG)
        return seen + jnp.sum(tie_f, axis=1, keepdims=True)
    lax.fori_loop(0, nc, bias_body, jnp.zeros((tq, 1), f32))

    heads = [slice(h * DH_D, (h + 1) * DH_D) for h in range(H_D)]
    qs = [q_ref[0, :, cols] for cols in heads]

    def att_body(c, carry):
        rows = pl.ds(pl.multiple_of(c * kc, kc), kc)
        bias = bias_ref[c]
        return tuple(_flash_chunk(qs[h], k_ref[0, rows, heads[h]], v_ref[0, rows, heads[h]], bias, scale, carry[h])
                     for h in range(H_D))
    fin = lax.fori_loop(0, nc, att_body, tuple(_flash_init(tq, DH_D) for _ in range(H_D)))
    for h in range(H_D):
        o_ref[0, :, heads[h]] = fin[h][2] / fin[h][1]


def _dsa_prompt_pallas(q, k, v, qi, ki, wi, n_sel, tq=128, kc=256):
    b, t, w = q.shape
    bf16 = jnp.bfloat16
    qh, ql = _split3(qi.reshape(b, t, H_I, D_I))
    kh, kl = _split3(ki)
    qi3 = jnp.concatenate([qh, qh, ql], -1).reshape(b, t, H_I * 3 * D_I)
    ki3 = jnp.concatenate([kh, kl, kh], -1)
    tri = jnp.asarray(np.triu(np.ones((kc, kc), np.float32), 1), bf16)
    kern = functools.partial(_dsa_prompt_kernel, tq=tq, kc=kc, n_sel=n_sel, scale=DH_D ** -0.5)
    blk = lambda bi, i: (bi, i, 0)
    full = lambda bi, i: (bi, 0, 0)
    return pl.pallas_call(
        kern,
        grid=(b, t // tq),
        in_specs=[pl.BlockSpec((1, tq, w), blk), pl.BlockSpec((1, t, w), full), pl.BlockSpec((1, t, w), full),
                  pl.BlockSpec((1, tq, H_I * 3 * D_I), blk), pl.BlockSpec((1, t, 3 * D_I), full),
                  pl.BlockSpec((1, tq, H_I), blk), pl.BlockSpec((kc, kc), lambda bi, i: (0, 0))],
        out_specs=pl.BlockSpec((1, tq, w), blk),
        out_shape=jax.ShapeDtypeStruct((b, t, w), jnp.float32),
        scratch_shapes=[pltpu.VMEM((t // kc, tq, kc), jnp.int32), pltpu.VMEM((t // kc, tq, kc), jnp.float32)],
        compiler_params=pltpu.CompilerParams(
            dimension_semantics=("arbitrary", "arbitrary"), vmem_limit_bytes=VMEM_LIMIT),
        name="dsa_prompt",
    )(q.astype(bf16), k.astype(bf16), v.astype(bf16), qi3, ki3, wi, tri)


def _diff_prompt_kernel(lam_ref, q_ref, k_ref, v_ref, o_ref, *, tq, kc, scale):
    i = pl.program_id(1)
    nc = (i * tq + tq + kc - 1) // kc
    qpos = i * tq + lax.broadcasted_iota(jnp.int32, (tq, 1), 0)
    lane = lax.broadcasted_iota(jnp.int32, (1, kc), 1)
    dcol = lax.broadcasted_iota(jnp.int32, (1, 2 * DH_B), 1)
    lam = lam_ref[0, 0]
    for pair in range(H_B // 2):
        heads = [slice(h * 2 * DH_B, (h + 1) * 2 * DH_B) for h in (2 * pair, 2 * pair + 1)]
        qms = [jnp.where((dcol >= m * DH_B) & (dcol < (m + 1) * DH_B), q_ref[0, :, cols], 0.0).astype(jnp.bfloat16)
               for cols in heads for m in range(2)]

        def att_body(c, carry, heads=heads, qms=qms):
            off = pl.multiple_of(c * kc, kc)
            rows = pl.ds(off, kc)
            bias = jnp.where(off + lane <= qpos, 0.0, NEG_BIG)
            out = []
            for j, cols in enumerate(heads):
                kh = k_ref[0, rows, cols]
                vh = v_ref[0, rows, cols]
                for m in range(2):
                    out.append(_flash_chunk(qms[2 * j + m], kh, vh, bias, scale, carry[2 * j + m]))
            return tuple(out)
        fin = lax.fori_loop(0, nc, att_body, tuple(_flash_init(tq, 2 * DH_B) for _ in range(4)))
        for j, cols in enumerate(heads):
            o_ref[0, :, cols] = fin[2 * j][2] / fin[2 * j][1] - lam * (fin[2 * j + 1][2] / fin[2 * j + 1][1])


def _diff_prompt_pallas(q, k, v, lam, tq=128, kc=256):
    b, t, w = q.shape
    bf16 = jnp.bfloat16
    kern = functools.partial(_diff_prompt_kernel, tq=tq, kc=kc, scale=DH_B ** -0.5)
    blk = lambda bi, i: (bi, i, 0)
    full = lambda bi, i: (bi, 0, 0)
    return pl.pallas_call(
        kern,
        grid=(b, t // tq),
        in_specs=[pl.BlockSpec(memory_space=pltpu.SMEM),
                  pl.BlockSpec((1, tq, w), blk), pl.BlockSpec((1, t, w), full), pl.BlockSpec((1, t, w), full)],
        out_specs=pl.BlockSpec((1, tq, w), blk),
        out_shape=jax.ShapeDtypeStruct((b, t, w), jnp.float32),
        compiler_params=pltpu.CompilerParams(
            dimension_semantics=("arbitrary", "arbitrary"), vmem_limit_bytes=VMEM_LIMIT),
        name="diff_prompt",
    )(lam.reshape(1, 1).astype(jnp.float32), q.astype(bf16), k.astype(bf16), v.astype(bf16))


PPS = 8
TPAD = 8
NEG_INF_KEY = INT_MIN + 0x7FFFFF


def _float_order_key(x):
    bits = pltpu.bitcast(x, jnp.int32)
    bits = jnp.where(bits == INT_MIN, 0, bits)
    return jnp.where(bits < 0, bits ^ 0x7FFFFFFF, bits)


def _page_spec(l, r, width, n_pages):
    def index(b, j, pt):
        return (l, pt[b * n_pages + j * PPS + r], 0, 0)
    return pl.BlockSpec((1, 1, PAGE_SIZE, width), index)


def _sample_score_kernel(pt_ref, qh_ref, ql_ref, w_ref, kn_ref, *rest):
    ki_refs, (o_ref, on_ref) = rest[:PPS], rest[PPS:]
    f32 = jnp.float32
    qh, ql, w = qh_ref[0], ql_ref[0], w_ref[0]

    def scores(kf):
        kh, kl = _split3(kf)
        s = (lax.dot_general(qh, kh, _NT, preferred_element_type=f32)
             + lax.dot_general(qh, kl, _NT, preferred_element_type=f32)
             + lax.dot_general(ql, kh, _NT, preferred_element_type=f32))
        r = w * jnp.maximum(s, 0.0)
        out = r[0:TPAD]
        for h in range(1, H_I):
            out = out + r[h * TPAD:(h + 1) * TPAD]
        return out
    for r in range(PPS):
        o_ref[0, :, r * PAGE_SIZE:(r + 1) * PAGE_SIZE] = scores(ki_refs[r][0, 0])

    @pl.when(pl.program_id(1) == 0)
    def _():
        t = lax.broadcasted_iota(jnp.int32, (TPAD, PAGE_SIZE), 0)
        j = lax.broadcasted_iota(jnp.int32, (TPAD, PAGE_SIZE), 1)
        on_ref[0] = jnp.where(j <= t, scores(kn_ref[0]), -jnp.inf)


def _sample_scores_pallas(l, qi, ki, wi, cache_ki, page_table):
    b, t = qi.shape[:2]
    n_pages = page_table.shape[1]
    qpad = jnp.zeros((b, H_I, TPAD, D_I), jnp.float32).at[:, :, :t].set(qi.transpose(0, 2, 1, 3))
    qh, ql = _split3(qpad.reshape(b, H_I * TPAD, D_I))
    wcol = jnp.zeros((b, H_I, TPAD), jnp.float32).at[:, :, :t].set(wi.transpose(0, 2, 1))
    wcol = (wcol * (H_I ** -0.5 * D_I ** -0.5)).reshape(b, H_I * TPAD, 1)
    knew = jnp.zeros((b, PAGE_SIZE, D_I), jnp.float32).at[:, :t].set(ki)
    row = lambda bi, j, pt: (bi, 0, 0)
    grid_spec = pltpu.PrefetchScalarGridSpec(
        num_scalar_prefetch=1, grid=(b, n_pages // PPS),
        in_specs=[pl.BlockSpec((1, H_I * TPAD, D_I), row), pl.BlockSpec((1, H_I * TPAD, D_I), row),
                  pl.BlockSpec((1, H_I * TPAD, 1), row), pl.BlockSpec((1, PAGE_SIZE, D_I), row)]
                 + [_page_spec(l, r, D_I, n_pages) for r in range(PPS)],
        out_specs=[pl.BlockSpec((1, TPAD, PPS * PAGE_SIZE), lambda bi, j, pt: (bi, 0, j)),
                   pl.BlockSpec((1, TPAD, PAGE_SIZE), row)])
    return pl.pallas_call(
        _sample_score_kernel, grid_spec=grid_spec,
        out_shape=[jax.ShapeDtypeStruct((b, TPAD, n_pages * PAGE_SIZE), jnp.float32),
                   jax.ShapeDtypeStruct((b, TPAD, PAGE_SIZE), jnp.float32)],
        compiler_params=pltpu.CompilerParams(dimension_semantics=("arbitrary", "arbitrary")),
        name="dsa_sample_scores",
    )(page_table.reshape(-1), qh, ql, wcol, knew, *([cache_ki] * PPS))


def _topk_bias_kernel(sc_ref, tri_ref, bias_ref, key_ref, *, n_sel):
    f32 = jnp.float32
    nc, rows, kc = sc_ref.shape

    def key_body(c, carry):
        key_ref[c] = _float_order_key(sc_ref[c])
        return carry
    lax.fori_loop(0, nc, key_body, 0)

    def count(pred, ref_val):
        def body(c, part):
            hit = jnp.where(pred(key_ref[c], ref_val), 1.0, 0.0)
            for j in range(kc // LANES):
                part = part + hit[:, j * LANES:(j + 1) * LANES]
            return part
        part = lax.fori_loop(0, nc, body, jnp.zeros((rows, LANES), f32))
        return jnp.sum(part, axis=1, keepdims=True)

    def bit_body(it, tau):
        cand = tau ^ jnp.left_shift(jnp.int32(1), 31 - it)
        return jnp.where(count(lambda a, b: a >= b, cand) >= n_sel, cand, tau)
    tau = lax.fori_loop(0, 32, bit_body, jnp.full((rows, 1), INT_MIN, jnp.int32))
    need = n_sel - count(lambda a, b: a > b, tau)

    def bias_body(c, seen):
        key = key_ref[c]
        tie = key == tau
        tie_f = jnp.where(tie, 1.0, 0.0)
        before = seen + jnp.dot(tie_f.astype(jnp.bfloat16), tri_ref[...], preferred_element_type=f32)
        sel = ((key > tau) | (tie & (before < need))) & (key > NEG_INF_KEY)
        bias_ref[c] = jnp.where(sel, 0.0, NEG_BIG)
        return seen + jnp.sum(tie_f, axis=1, keepdims=True)
    lax.fori_loop(0, nc, bias_body, jnp.zeros((rows, 1), f32))


def _topk_bias_pallas(scores, n_sel, kc=256):
    rows, n = scores.shape
    nc = n // kc
    tri = jnp.asarray(np.triu(np.ones((kc, kc), np.float32), 1), jnp.bfloat16)
    bias = pl.pallas_call(
        functools.partial(_topk_bias_kernel, n_sel=n_sel),
        out_shape=jax.ShapeDtypeStruct((nc, rows, kc), jnp.float32),
        scratch_shapes=[pltpu.VMEM((nc, rows, kc), jnp.int32)],
        compiler_params=pltpu.CompilerParams(vmem_limit_bytes=VMEM_LIMIT),
        name="topk_bias",
    )(scores.reshape(rows, nc, kc).transpose(1, 0, 2), tri)
    return bias.transpose(1, 0, 2).reshape(rows, n)


def _paged_attn_kernel(pt_ref, q_ref, kn_ref, vn_ref, bn_ref, *rest, scale, has_bias):
    f32 = jnp.float32
    bf16 = jnp.bfloat16
    if has_bias:
        bp_ref, rest = rest[0], rest[1:]
    k_refs, v_refs = rest[:PPS], rest[PPS:2 * PPS]
    o_ref, m_ref, l_ref, acc_ref = rest[2 * PPS:]
    j = pl.program_id(1)

    @pl.when(j == 0)
    def _():
        m_ref[...] = jnp.full(m_ref.shape, NEG_BIG, f32)
        l_ref[...] = jnp.zeros(l_ref.shape, f32)
        acc_ref[...] = jnp.zeros(acc_ref.shape, f32)

    q = q_ref[0]

    def update(s_list, v_list):
        m_old = m_ref[...]
        m_new = m_old
        for s in s_list:
            m_new = jnp.maximum(m_new, jnp.max(s, axis=1, keepdims=True))
        alpha = jnp.exp(m_old - m_new)
        l = alpha * l_ref[...]
        acc = alpha * acc_ref[...]
        for s, v in zip(s_list, v_list):
            p = jnp.exp(s - m_new)
            l = l + jnp.sum(p, axis=1, keepdims=True)
            acc = acc + jnp.dot(p.astype(bf16), v, preferred_element_type=f32)
        m_ref[...] = m_new
        l_ref[...] = l
        acc_ref[...] = acc

    s_list, v_list = [], []
    for r in range(PPS):
        s = lax.dot_general(q, k_refs[r][0, 0].astype(bf16), _NT, preferred_element_type=f32) * scale
        if has_bias:
            s = s + bp_ref[0, :, r * PAGE_SIZE:(r + 1) * PAGE_SIZE]
        s_list.append(s)
        v_list.append(v_refs[r][0, 0].astype(bf16))
    update(s_list, v_list)

    @pl.when(j == pl.num_programs(1) - 1)
    def _():
        s = lax.dot_general(q, kn_ref[0], _NT, preferred_element_type=f32) * scale + bn_ref[0]
        update([s], [vn_ref[0]])
        o_ref[0] = acc_ref[...] / l_ref[...]


def _paged_attn_pallas(l, qbd, k_new, v_new, bias_new, bias_past, cache_k, cache_v, page_table, scale):
    b, rws, w = qbd.shape
    t = k_new.shape[1]
    n_pages = page_table.shape[1]
    bf16 = jnp.bfloat16
    kn = jnp.zeros((b, TPAD, w), bf16).at[:, :t].set(k_new.astype(bf16))
    vn = jnp.zeros((b, TPAD, w), bf16).at[:, :t].set(v_new.astype(bf16))
    bn = jnp.full((b, rws, TPAD), NEG_BIG, jnp.float32).at[:, :, :t].set(bias_new)
    row = lambda bi, j, pt: (bi, 0, 0)
    in_specs = [pl.BlockSpec((1, rws, w), row), pl.BlockSpec((1, TPAD, w), row), pl.BlockSpec((1, TPAD, w), row),
                pl.BlockSpec((1, rws, TPAD), row)]
    args = [qbd.astype(bf16), kn, vn, bn]
    if bias_past is not None:
        in_specs.append(pl.BlockSpec((1, rws, PPS * PAGE_SIZE), lambda bi, j, pt: (bi, 0, j)))
        args.append(bias_past)
    in_specs += [_page_spec(l, r, w, n_pages) for r in range(PPS)] * 2
    args += [cache_k] * PPS + [cache_v] * PPS
    grid_spec = pltpu.PrefetchScalarGridSpec(
        num_scalar_prefetch=1, grid=(b, n_pages // PPS), in_specs=in_specs,
        out_specs=pl.BlockSpec((1, rws, w), row),
        scratch_shapes=[pltpu.VMEM((rws, 1), jnp.float32), pltpu.VMEM((rws, 1), jnp.float32),
                        pltpu.VMEM((rws, w), jnp.float32)])
    return pl.pallas_call(
        functools.partial(_paged_attn_kernel, scale=scale, has_bias=bias_past is not None),
        grid_spec=grid_spec,
        out_shape=jax.ShapeDtypeStruct((b, rws, w), jnp.float32),
        compiler_params=pltpu.CompilerParams(
            dimension_semantics=("arbitrary", "arbitrary"), vmem_limit_bytes=VMEM_LIMIT),
        name="paged_attn",
    )(page_table.reshape(-1), *args)


def _head_rows(x, n_blocks):
    b, t, w = x.shape
    blk = jnp.arange(w) // (w // n_blocks)
    keep = blk[None, :] == jnp.arange(n_blocks)[:, None]
    return jnp.where(keep[None, :, None, :], x[:, None, :, :], 0.0).reshape(b, n_blocks * t, w)


def _dsa_sample_pallas(l, q, k, v, qi, ki, wi_raw, cache_ki, cache_k, cache_v, page_table):
    b, t, h, dh = q.shape
    p_len = page_table.shape[1] * PAGE_SIZE
    n_sel = min(TOPK_MAX, (p_len + t) // 4)
    s_past, s_new = _sample_scores_pallas(l, qi, ki, wi_raw, cache_ki, page_table)
    pad = jnp.full((b, t, PAGE_SIZE), -jnp.inf, jnp.float32)
    scores = jnp.concatenate([s_past[:, :t], s_new[:, :t], pad], -1).reshape(b * t, p_len + 2 * PAGE_SIZE)
    bias = _topk_bias_pallas(scores, n_sel).reshape(b, t, -1)
    tile = lambda z: jnp.tile(z, (1, h, 1))
    w = h * dh
    res = _paged_attn_pallas(l, _head_rows(q.reshape(b, t, w), h), k.reshape(b, t, w), v.reshape(b, t, w),
                             tile(bias[:, :, p_len:p_len + t]), tile(bias[:, :, :p_len]),
                             cache_k.reshape(cache_k.shape[:3] + (w,)), cache_v.reshape(cache_v.shape[:3] + (w,)),
                             page_table, dh ** -0.5)
    res = res.reshape(b, h, t, h, dh)
    return jnp.stack([res[:, i, :, i] for i in range(h)], 2)


def _diff_sample_pallas(l, q, k, v, cache_k, cache_v, page_table, lam):
    b, t, h = q.shape[:3]
    w = h * 2 * DH_B
    causal = jnp.where(jnp.arange(t)[None, :] <= jnp.arange(t)[:, None], 0.0, NEG_BIG)
    bias_new = jnp.broadcast_to(jnp.tile(causal, (2 * h, 1))[None], (b, 2 * h * t, t))
    res = _paged_attn_pallas(l, _head_rows(q.reshape(b, t, w), 2 * h), k.reshape(b, t, w), v.reshape(b, t, w),
                             bias_new, None,
                             cache_k.reshape(cache_k.shape[:3] + (w,)), cache_v.reshape(cache_v.shape[:3] + (w,)),
                             page_table, DH_B ** -0.5)
    res = res.reshape(b, h, 2, t, h, 2 * DH_B)
    return jnp.stack([res[:, i, 0, :, i] - lam * res[:, i, 1, :, i] for i in range(h)], 2)


def _split_in(u):
    parts = {}
    off = 0
    for name, width in IN_WIDTHS:
        parts[name] = u[..., off:off + width]
        off += width
    return parts


def _rope(x, pos):
    half = x.shape[-1] // 2
    inv = ROPE_THETA ** (-jnp.arange(half, dtype=jnp.float32) / half)
    ang = pos.astype(jnp.float32)[:, None] * inv[None, :]
    cos = jnp.cos(ang)[:, None, :]
    sin = jnp.sin(ang)[:, None, :]
    x1 = x[..., :half].astype(jnp.float32)
    x2 = x[..., half:].astype(jnp.float32)
    return jnp.concatenate([x1 * cos - x2 * sin, x1 * sin + x2 * cos], -1).astype(x.dtype)


def _head_rms(h, g):
    hf = h.astype(jnp.float32)
    hf = hf * lax.rsqrt(jnp.mean(hf * hf, -1, keepdims=True) + NORM_EPS)
    return (hf.reshape(h.shape[:-2] + (-1,)) * g).astype(h.dtype)


def _layernorm(x, g, b):
    xf = x.astype(jnp.float32)
    xc = xf - jnp.mean(xf, -1, keepdims=True)
    var = jnp.mean(xc * xc, -1, keepdims=True)
    return (xc * lax.rsqrt(var + LN_EPS) * g + b).astype(x.dtype)


def _causal_conv(xs, buf, w):
    t = xs.shape[1]
    full = jnp.concatenate([buf.astype(xs.dtype), xs], 1)
    out = full[:, 0:t] * w[0]
    for j in range(1, CONV_W):
        out = out + full[:, j:j + t] * w[j]
    return out, full[:, t:]


def _gla(q, k, v, log_a, s0):
    b, t, h, dk = q.shape
    dt = v.dtype
    c = math.gcd(t, GLA_CHUNK)
    n = t // c
    f32 = jnp.float32
    def chunks(z):
        return z.astype(f32).reshape(b, n, c, h, -1).transpose(1, 0, 3, 2, 4)
    mask = jnp.tril(jnp.ones((c, c), bool))
    def step(s, inp):
        qb, kb, vb, ab = inp
        cb = jnp.cumsum(ab, axis=2)
        o_inter = jnp.einsum('bhtk,bhkv->bhtv', qb * jnp.exp(cb), s)
        diff = cb[:, :, :, None, :] - cb[:, :, None, :, :]
        decay = jnp.exp(jnp.where(mask[:, :, None], diff, -jnp.inf))
        att = jnp.einsum('bhtk,bhsk,bhtsk->bhts', qb, kb, decay)
        o = o_inter + att @ vb
        c_last = cb[:, :, -1:, :]
        s_new = jnp.exp(c_last[:, :, 0, :])[..., None] * s + jnp.einsum('bhsk,bhsv->bhkv', kb * jnp.exp(c_last - cb), vb)
        return s_new, o
    s_fin, o = lax.scan(step, s0.astype(f32), (chunks(q * (dk ** -0.5)), chunks(k), chunks(v), chunks(log_a)))
    o = o.transpose(1, 0, 3, 2, 4).reshape(b, t, h, -1)
    return o.astype(dt), s_fin.astype(dt)


def _mlstm(q, k, v, i_pre, f_pre, c0, n0, m0):
    b, t, h, d = q.shape
    dt = v.dtype
    f32 = jnp.float32
    c = math.gcd(t, MLSTM_CHUNK)
    n = t // c
    def chunks(z):
        return z.astype(f32).reshape(b, n, c, h, -1).transpose(1, 0, 3, 2, 4)
    def gchunks(z):
        return z.astype(f32).reshape(b, n, c, h).transpose(1, 0, 3, 2)
    mask = jnp.tril(jnp.ones((c, c), bool))
    def step(carry, inp):
        cm, nv, m = carry
        qb, kb, vb, ib, fb = inp
        fcum = jnp.cumsum(jax.nn.log_sigmoid(fb), -1)
        dmat = jnp.where(mask, fcum[..., :, None] - fcum[..., None, :] + ib[..., None, :], -jnp.inf)
        inter = fcum + m[..., None]
        m_t = jnp.maximum(inter, jnp.max(dmat, -1))
        w_inter = jnp.exp(inter - m_t)
        qk = jnp.einsum('bhtd,bhsd->bhts', qb, kb) * jnp.exp(dmat - m_t[..., None])
        num = w_inter[..., None] * jnp.einsum('bhtd,bhde->bhte', qb, cm) + qk @ vb
        den = w_inter * jnp.einsum('bhtd,bhd->bht', qb, nv) + jnp.sum(qk, -1)
        hv = num / jnp.maximum(jnp.abs(den), jnp.exp(-m_t))[..., None]
        f_last = fcum[..., -1]
        g_s = f_last[..., None] - fcum + ib
        m_new = jnp.maximum(f_last + m, jnp.max(g_s, -1))
        a = jnp.exp(f_last + m - m_new)
        ws = jnp.exp(g_s - m_new[..., None])
        c_new = a[..., None, None] * cm + jnp.einsum('bhs,bhsd,bhse->bhde', ws, kb, vb)
        n_new = a[..., None] * nv + jnp.einsum('bhs,bhsd->bhd', ws, kb)
        return (c_new, n_new, m_new), hv
    (c_f, n_f, m_f), hs = lax.scan(step, (c0.astype(f32), n0.astype(f32), m0.astype(f32)),
                                   (chunks(q), chunks(k), chunks(v), gchunks(i_pre), gchunks(f_pre)))
    hs = hs.transpose(1, 0, 3, 2, 4).reshape(b, t, h, d)
    return hs.astype(dt), c_f.astype(dt), n_f.astype(dt), m_f.astype(dt)


def _diff_attn_prompt(q, k, v, lam):
    b, s, h, _, d = q.shape
    nb = s // Q_BLOCK
    qb = q.reshape(b, nb, Q_BLOCK, h, 2, d).swapaxes(0, 1)
    kpos = jnp.arange(s)
    def one_block(args):
        qi, i = args
        qpos = i * Q_BLOCK + jnp.arange(Q_BLOCK)
        sc = jnp.einsum('bthmd,bshmd->bhmts', qi, k).astype(jnp.float32) * (d ** -0.5)
        sc = jnp.where(kpos[None, :] <= qpos[:, None], sc, -jnp.inf)
        p = jax.nn.softmax(sc, -1)
        a = (p[:, :, 0] - lam * p[:, :, 1]).astype(v.dtype)
        return jnp.einsum('bhts,bshe->bthe', a, v)
    o = lax.map(one_block, (qb, jnp.arange(nb)))
    return o.swapaxes(0, 1).reshape(b, s, h, 2 * d)


def _diff_attn_sample(q, k, v, k_past, v_past, lam):
    t = q.shape[1]
    d = q.shape[-1]
    p_len = k_past.shape[1]
    causal = jnp.tril(jnp.ones((t, t), bool))
    s_p = jnp.einsum('bthmd,bshmd->bhmts', q, k_past).astype(jnp.float32)
    s_n = jnp.where(causal, jnp.einsum('bthmd,bshmd->bhmts', q, k).astype(jnp.float32), -jnp.inf)
    p = jax.nn.softmax(jnp.concatenate([s_p, s_n], -1) * (d ** -0.5), -1)
    a = (p[:, :, 0] - lam * p[:, :, 1]).astype(v.dtype)
    return jnp.einsum('bhts,bshe->bthe', a[..., :p_len], v_past) + jnp.einsum('bhts,bshe->bthe', a[..., p_len:], v)


def _index_scores(qi, ki, wi):
    sc = jnp.einsum('bthi,bsi->bths', qi, ki).astype(jnp.float32) * (D_I ** -0.5)
    return jnp.einsum('bths,bth->bts', jax.nn.relu(sc), wi.astype(jnp.float32))


def _dsa_prompt(q, k, v, qi, ki, wi):
    b, s, h, dh = q.shape
    n_sel = min(TOPK_MAX, s // 4)
    nb = s // Q_BLOCK
    def blk(z):
        return z.reshape((b, nb, Q_BLOCK) + z.shape[2:]).swapaxes(0, 1)
    kpos = jnp.arange(s)
    gather = jax.vmap(lambda rows, sel: rows[sel])
    def one_block(args):
        qb, qib, wib, i = args
        qpos = i * Q_BLOCK + jnp.arange(Q_BLOCK)
        score = jnp.where(kpos[None, :] <= qpos[:, None], _index_scores(qib, ki, wib), -jnp.inf)
        _, sel = lax.top_k(score, n_sel)
        kg = gather(k, sel)
        vg = gather(v, sel)
        att = jnp.einsum('bthd,btkhd->bhtk', qb, kg).astype(jnp.float32) * (dh ** -0.5)
        valid = sel <= qpos[None, :, None]
        p = jax.nn.softmax(jnp.where(valid[:, None], att, -jnp.inf), -1)
        return jnp.einsum('bhtk,btkhd->bthd', p.astype(v.dtype), vg)
    o = lax.map(one_block, (blk(q), blk(qi), blk(wi), jnp.arange(nb)))
    return o.swapaxes(0, 1).reshape(b, s, h, dh)


def _dsa_sample(q, k, v, qi, ki, wi, ki_past, pool_k, pool_v, page_table):
    b, t, h, dh = q.shape
    p_len = ki_past.shape[1]
    n_sel = min(TOPK_MAX, (p_len + t) // 4)
    causal = jnp.tril(jnp.ones((t, t), bool))
    score = jnp.concatenate([_index_scores(qi, ki_past, wi),
                             jnp.where(causal, _index_scores(qi, ki, wi), -jnp.inf)], -1)
    _, sel = lax.top_k(score, n_sel)
    ps = jnp.minimum(sel, p_len - 1)
    phys = jnp.take_along_axis(page_table, (ps // PAGE_SIZE).reshape(b, -1), axis=1).reshape(sel.shape)
    flat = phys * PAGE_SIZE + ps % PAGE_SIZE
    ns = jnp.clip(sel - p_len, 0, t - 1)
    gather = jax.vmap(lambda rows, idx: rows[idx])
    is_past = (sel < p_len)[..., None, None]
    kg = jnp.where(is_past, pool_k.reshape((-1, h, dh))[flat], gather(k, ns))
    vg = jnp.where(is_past, pool_v.reshape((-1, h, dh))[flat], gather(v, ns))
    att = jnp.einsum('bthd,btkhd->bhtk', q, kg).astype(jnp.float32) * (dh ** -0.5)
    valid = sel <= (p_len + jnp.arange(t))[None, :, None]
    p = jax.nn.softmax(jnp.where(valid[:, None], att, -jnp.inf), -1)
    return jnp.einsum('bhtk,btkhd->bthd', p.astype(v.dtype), vg)


def _sublayer(x, pos, l, w_in_p, w_a_gate, b_a_gate, g_a, lam_qk, g_b, b_c_if, w_c_conv, g_c,
              w_branch, w_out, rec, paged):
    b, t, _ = x.shape
    f32 = jnp.float32
    a_s0, c_c0, c_n0, c_m0, c_conv0 = rec
    m = b * t
    u_full = _matmul(x.reshape(m, D_MODEL).astype(jnp.bfloat16), w_in_p[l], min(m, 512), 1536)
    u = _split_in(u_full[:, :N_IN].reshape(b, t, N_IN))
    qa = u['a_q'].reshape(b, t, H_A, DK_A)
    ka = u['a_k'].reshape(b, t, H_A, DK_A)
    va = u['a_v'].reshape(b, t, H_A, DV_A)
    log_a = jax.nn.log_sigmoid((u['a_g'] @ w_a_gate[l] + b_a_gate[l]).astype(f32)) / GATE_TAU
    oa, a_s = _gla(qa, ka, va, log_a.reshape(b, t, H_A, DK_A), a_s0)
    ya = _head_rms(oa, g_a[l]) * jax.nn.silu(u['a_z'])
    qb = _rope(u['b_q'].reshape(b, t, 2 * H_B, DH_B), pos).reshape(b, t, H_B, 2, DH_B)
    kb = _rope(u['b_k'].reshape(b, t, 2 * H_B, DH_B), pos).reshape(b, t, H_B, 2, DH_B)
    vb = u['b_v'].reshape(b, t, H_B, 2 * DH_B)
    lam_init = 0.8 - 0.6 * math.exp(-0.3 * l)
    lq = lam_qk[l].astype(f32)
    lam = jnp.exp(jnp.sum(lq[0] * lq[1])) - jnp.exp(jnp.sum(lq[2] * lq[3])) + lam_init
    if paged is None:
        ob = _diff_prompt_pallas(qb.reshape(b, t, W_BR), kb.reshape(b, t, W_BR), vb.reshape(b, t, W_BR),
                                 lam).reshape(b, t, H_B, 2 * DH_B)
    else:
        page_table, pb_k, pb_v, pd_k, pd_v, pd_ki = paged
        ob = _diff_sample_pallas(l, qb, kb, vb, pb_k, pb_v, page_table, lam)
    yb = _head_rms(ob, g_b[l]) * (1.0 - lam_init) * jax.nn.silu(u['b_z'])
    qk_pre = jnp.concatenate([u['c_q'], u['c_k']], -1)
    qk_conv, c_conv = _causal_conv(qk_pre, c_conv0, w_c_conv[l])
    qk_conv = jax.nn.silu(qk_conv)
    qc = qk_conv[..., :W_BR].reshape(b, t, H_C, DH_C)
    kc = qk_conv[..., W_BR:].reshape(b, t, H_C, DH_C) * (DH_C ** -0.5)
    vc = u['c_v'].reshape(b, t, H_C, DH_C)
    i_pre = u['c_i'].astype(f32) + b_c_if[l, 0]
    f_pre = u['c_f'].astype(f32) + b_c_if[l, 1]
    hc, c_c, c_n, c_m = _mlstm(qc, kc, vc, i_pre, f_pre, c_c0, c_n0, c_m0)
    hc = jax.nn.sigmoid(u['c_o']).reshape(b, t, H_C, DH_C) * hc
    yc = _head_rms(hc, g_c[l]) * jax.nn.silu(u['c_z'])
    qd = _rope(u['d_q'].reshape(b, t, H_D, DH_D), pos)
    kd = _rope(u['d_k'].reshape(b, t, H_D, DH_D), pos)
    vd = u['d_v'].reshape(b, t, H_D, DH_D)
    qi = _rope(u['d_qi'].reshape(b, t, H_I, D_I), pos)
    ki = _rope(u['d_ki'].reshape(b, t, 1, D_I), pos)[:, :, 0]
    wi = u['d_w'] * (H_I ** -0.5)
    if paged is None:
        od = _dsa_prompt_pallas(qd.reshape(b, t, W_BR), kd.reshape(b, t, W_BR), vd.reshape(b, t, W_BR),
                                qi.reshape(b, t, H_I * D_I), ki, u['d_w'], min(TOPK_MAX, t // 4))
    else:
        od = _dsa_sample_pallas(l, qd, kd, vd, qi, ki, u['d_w'], pd_ki, pd_k, pd_v, page_table)
    yd = od.reshape(b, t, W_BR) * jax.nn.silu(u['d_z'])
    ys = jnp.stack([ya, yb, yc, yd], 2)
    proj = jnp.einsum('btnw,nwd->btnd', ys, w_branch[l])
    gates = jax.nn.sigmoid(u['gate'].reshape(b, t, N_BRANCH, D_MODEL))
    out = jnp.einsum('btd,de->bte', jnp.sum(gates * proj, 2), w_out[l])
    return out, (kb, vb, kd, vd, ki, a_s, c_c, c_n, c_m, c_conv)


def kernel(x_prompt, x_sample, cache_b_k, cache_b_v, cache_d_k, cache_d_v, cache_d_ki,
           state_a_s, state_c_c, state_c_n, state_c_m, state_c_conv, page_table,
           w_in, w_a_gate, b_a_gate, g_a, lam_qk, g_b, b_c_if, w_c_conv, g_c,
           w_branch, w_out, ln_g, ln_b):
    bp, tp, _ = x_prompt.shape
    bd, td, _ = x_sample.shape
    past = page_table.shape[1] * PAGE_SIZE
    pos_p = jnp.arange(tp)
    pos_s = past + jnp.arange(td)
    dt = x_prompt.dtype
    w_in_p = jnp.pad(w_in.astype(jnp.bfloat16), ((0, 0), (0, 0), (0, N_IN_PAD - N_IN)))
    hp = x_prompt
    hs = x_sample
    new_p = []
    new_s = []
    for l in range(DEPTH):
        rec_p = (jnp.zeros((bp, H_A, DK_A, DV_A), dt), jnp.zeros((bp, H_C, DH_C, DH_C), dt),
                 jnp.zeros((bp, H_C, DH_C), dt), jnp.zeros((bp, H_C), dt),
                 jnp.zeros((bp, CONV_W - 1, 2 * W_BR), dt))
        out_p, st_p = _sublayer(hp, pos_p, l, w_in_p, w_a_gate, b_a_gate, g_a, lam_qk, g_b, b_c_if,
                                w_c_conv, g_c, w_branch, w_out, rec_p, None)
        hp = _layernorm(DN_ALPHA * hp + out_p, ln_g[l], ln_b[l])
        rec_s = (state_a_s[l], state_c_c[l], state_c_n[l], state_c_m[l], state_c_conv[l])
        paged = (page_table, cache_b_k, cache_b_v, cache_d_k, cache_d_v, cache_d_ki)
        out_s, st_s = _sublayer(hs, pos_s, l, w_in_p, w_a_gate, b_a_gate, g_a, lam_qk, g_b, b_c_if,
                                w_c_conv, g_c, w_branch, w_out, rec_s, paged)
        hs = _layernorm(DN_ALPHA * hs + out_s, ln_g[l], ln_b[l])
        new_p.append(st_p)
        new_s.append(st_s)
    outs_p = [jnp.stack([st[i] for st in new_p], 0) for i in range(10)]
    outs_s = [jnp.stack([st[i] for st in new_s], 0) for i in range(10)]
    return (hp, hs, *outs_p, *outs_s)
```

```python
import functools
import math
import jax
import jax.numpy as jnp
from jax import lax
import numpy as np
from jax.experimental import pallas as pl
from jax.experimental.pallas import tpu as pltpu

D_MODEL = 2048
DEPTH = 2
PAGE_SIZE = 128
N_BRANCH = 4
W_BR = D_MODEL // 4
H_A = 4
DK_A = W_BR // (2 * H_A)
DV_A = W_BR // H_A
GATE_RANK = 16
GATE_TAU = 16.0
GLA_CHUNK = 16
H_B = 4
DH_B = W_BR // (2 * H_B)
H_C = 4
DH_C = W_BR // H_C
CONV_W = 4
MLSTM_CHUNK = 64
H_D = 4
DH_D = W_BR // H_D
H_I = 4
D_I = 64
TOPK_MAX = 256
Q_BLOCK = 128
ROPE_THETA = 10000.0
LN_EPS = 1e-5
NORM_EPS = 1e-6
DN_ALPHA = (2.0 * DEPTH) ** 0.25

IN_WIDTHS = (
    ('a_q', H_A * DK_A), ('a_k', H_A * DK_A), ('a_v', W_BR), ('a_g', GATE_RANK), ('a_z', W_BR),
    ('b_q', W_BR), ('b_k', W_BR), ('b_v', W_BR), ('b_z', W_BR),
    ('c_q', W_BR), ('c_k', W_BR), ('c_v', W_BR), ('c_i', H_C), ('c_f', H_C), ('c_o', W_BR), ('c_z', W_BR),
    ('d_q', W_BR), ('d_k', W_BR), ('d_v', W_BR), ('d_qi', H_I * D_I), ('d_ki', D_I), ('d_w', H_I), ('d_z', W_BR),
    ('gate', N_BRANCH * D_MODEL),
)
N_IN = sum(w for _, w in IN_WIDTHS)
N_MAIN = N_IN - N_BRANCH * D_MODEL
LANES = 128
N_MAIN_PAD = -(-N_MAIN // (4 * LANES)) * (4 * LANES)


def _mm_kernel(x_ref, w_ref, o_ref):
    o_ref[...] = jnp.dot(x_ref[...], w_ref[...], preferred_element_type=jnp.float32)


def _matmul(x, w, tm, tn):
    m, k = x.shape
    n = w.shape[1]
    return pl.pallas_call(
        _mm_kernel,
        grid=(n // tn, m // tm),
        in_specs=[pl.BlockSpec((tm, k), lambda j, i: (i, 0)),
                  pl.BlockSpec((k, tn), lambda j, i: (0, j))],
        out_specs=pl.BlockSpec((tm, tn), lambda j, i: (i, j)),
        out_shape=jax.ShapeDtypeStruct((m, n), jnp.float32),
        compiler_params=pltpu.CompilerParams(
            dimension_semantics=("arbitrary", "arbitrary"), vmem_limit_bytes=48 * 1024 * 1024),
        name="in_proj",
    )(x, w)


NEG_BIG = -1e30
INT_MIN = -2 ** 31
_NT = (((1,), (1,)), ((), ()))
VMEM_LIMIT = 48 * 1024 * 1024


def _split3(x):
    hi = x.astype(jnp.bfloat16)
    lo = (x - hi.astype(jnp.float32)).astype(jnp.bfloat16)
    return hi, lo


def _flash_chunk(qh, kh, vh, bias, scale, carry):
    m, l, acc = carry
    s = lax.dot_general(qh, kh, _NT, preferred_element_type=jnp.float32) * scale + bias
    m_new = jnp.maximum(m, jnp.max(s, axis=1, keepdims=True))
    alpha = jnp.exp(m - m_new)
    p = jnp.exp(s - m_new)
    l = alpha * l + jnp.sum(p, axis=1, keepdims=True)
    acc = alpha * acc + jnp.dot(p.astype(jnp.bfloat16), vh, preferred_element_type=jnp.float32)
    return m_new, l, acc


def _flash_init(tq, dv):
    return (jnp.full((tq, 1), NEG_BIG, jnp.float32), jnp.zeros((tq, 1), jnp.float32),
            jnp.zeros((tq, dv), jnp.float32))


def _dsa_prompt_kernel(q_ref, k_ref, v_ref, qi_ref, ki_ref, wi_ref, tri_ref, o_ref, key_ref, bias_ref,
                       *, tq, kc, n_sel, scale):
    f32 = jnp.float32
    i = pl.program_id(1)
    nc = (i * tq + tq + kc - 1) // kc
    qpos = i * tq + lax.broadcasted_iota(jnp.int32, (tq, 1), 0)
    lane = lax.broadcasted_iota(jnp.int32, (1, kc), 1)
    wi = wi_ref[0] * (H_I ** -0.5 * D_I ** -0.5)
    wis = [jnp.broadcast_to(wi[:, h:h + 1], (tq, kc)) for h in range(H_I)]
    d3 = 3 * D_I

    def score_body(c, carry):
        off = pl.multiple_of(c * kc, kc)
        kik = ki_ref[0, pl.ds(off, kc), :]
        acc = jnp.zeros((tq, kc), f32)
        for h in range(H_I):
            s = lax.dot_general(qi_ref[0, :, h * d3:(h + 1) * d3], kik, _NT, preferred_element_type=f32)
            acc = acc + wis[h] * jnp.maximum(s, 0.0)
        acc = jnp.where(off + lane <= qpos, acc, -jnp.inf)
        bits = pltpu.bitcast(acc, jnp.int32)
        bits = jnp.where(bits == INT_MIN, 0, bits)
        key_ref[c] = jnp.where(bits < 0, bits ^ 0x7FFFFFFF, bits)
        return carry
    lax.fori_loop(0, nc, score_body, 0)

    def count(pred, ref_val):
        def body(c, part):
            hit = jnp.where(pred(key_ref[c], ref_val), 1.0, 0.0)
            for j in range(kc // LANES):
                part = part + hit[:, j * LANES:(j + 1) * LANES]
            return part
        part = lax.fori_loop(0, nc, body, jnp.zeros((tq, LANES), f32))
        return jnp.sum(part, axis=1, keepdims=True)

    def bit_body(it, tau):
        cand = tau ^ jnp.left_shift(jnp.int32(1), 31 - it)
        return jnp.where(count(lambda a, b: a >= b, cand) >= n_sel, cand, tau)
    tau = lax.fori_loop(0, 32, bit_body, jnp.full((tq, 1), INT_MIN, jnp.int32))
    need = n_sel - count(lambda a, b: a > b, tau)

    def bias_body(c, seen):
        key = key_ref[c]
        tie = key == tau
        tie_f = jnp.where(tie, 1.0, 0.0)
        before = seen + jnp.dot(tie_f.astype(jnp.bfloat16), tri_ref[...], preferred_element_type=f32)
        sel = ((key > tau) | (tie & (before < need))) & (c * kc + lane <= qpos)
        bias_ref[c] = jnp.where(sel, 0.0, NEG_BIG)
        return seen + jnp.sum(tie_f, axis=1, keepdims=True)
    lax.fori_loop(0, nc, bias_body, jnp.zeros((tq, 1), f32))

    heads = [slice(h * DH_D, (h + 1) * DH_D) for h in range(H_D)]
    qs = [q_ref[0, :, cols] for cols in heads]

    def att_body(c, carry):
        rows = pl.ds(pl.multiple_of(c * kc, kc), kc)
        bias = bias_ref[c]
        return tuple(_flash_chunk(qs[h], k_ref[0, rows, heads[h]], v_ref[0, rows, heads[h]], bias, scale, carry[h])
                     for h in range(H_D))
    fin = lax.fori_loop(0, nc, att_body, tuple(_flash_init(tq, DH_D) for _ in range(H_D)))
    for h in range(H_D):
        o_ref[0, :, heads[h]] = fin[h][2] / fin[h][1]


def _dsa_prompt_pallas(q, k, v, qi, ki, wi, n_sel, tq=128, kc=256):
    b, t, w = q.shape
    bf16 = jnp.bfloat16
    qh, ql = _split3(qi.reshape(b, t, H_I, D_I))
    kh, kl = _split3(ki)
    qi3 = jnp.concatenate([qh, qh, ql], -1).reshape(b, t, H_I * 3 * D_I)
    ki3 = jnp.concatenate([kh, kl, kh], -1)
    tri = jnp.asarray(np.triu(np.ones((kc, kc), np.float32), 1), bf16)
    kern = functools.partial(_dsa_prompt_kernel, tq=tq, kc=kc, n_sel=n_sel, scale=DH_D ** -0.5)
    blk = lambda bi, i: (bi, i, 0)
    full = lambda bi, i: (bi, 0, 0)
    return pl.pallas_call(
        kern,
        grid=(b, t // tq),
        in_specs=[pl.BlockSpec((1, tq, w), blk), pl.BlockSpec((1, t, w), full), pl.BlockSpec((1, t, w), full),
                  pl.BlockSpec((1, tq, H_I * 3 * D_I), blk), pl.BlockSpec((1, t, 3 * D_I), full),
                  pl.BlockSpec((1, tq, H_I), blk), pl.BlockSpec((kc, kc), lambda bi, i: (0, 0))],
        out_specs=pl.BlockSpec((1, tq, w), blk),
        out_shape=jax.ShapeDtypeStruct((b, t, w), jnp.float32),
        scratch_shapes=[pltpu.VMEM((t // kc, tq, kc), jnp.int32), pltpu.VMEM((t // kc, tq, kc), jnp.float32)],
        compiler_params=pltpu.CompilerParams(
            dimension_semantics=("arbitrary", "arbitrary"), vmem_limit_bytes=VMEM_LIMIT),
        name="dsa_prompt",
    )(q.astype(bf16), k.astype(bf16), v.astype(bf16), qi3, ki3, wi, tri)


def _diff_prompt_kernel(lam_ref, q_ref, k_ref, v_ref, o_ref, *, tq, kc, scale):
    i = pl.program_id(1)
    nc = (i * tq + tq + kc - 1) // kc
    qpos = i * tq + lax.broadcasted_iota(jnp.int32, (tq, 1), 0)
    lane = lax.broadcasted_iota(jnp.int32, (1, kc), 1)
    dcol = lax.broadcasted_iota(jnp.int32, (1, 2 * DH_B), 1)
    lam = lam_ref[0, 0]
    for pair in range(H_B // 2):
        heads = [slice(h * 2 * DH_B, (h + 1) * 2 * DH_B) for h in (2 * pair, 2 * pair + 1)]
        qms = [jnp.where((dcol >= m * DH_B) & (dcol < (m + 1) * DH_B), q_ref[0, :, cols], 0.0).astype(jnp.bfloat16)
               for cols in heads for m in range(2)]

        def att_body(c, carry, heads=heads, qms=qms):
            off = pl.multiple_of(c * kc, kc)
            rows = pl.ds(off, kc)
            bias = jnp.where(off + lane <= qpos, 0.0, NEG_BIG)
            out = []
            for j, cols in enumerate(heads):
                kh = k_ref[0, rows, cols]
                vh = v_ref[0, rows, cols]
                for m in range(2):
                    out.append(_flash_chunk(qms[2 * j + m], kh, vh, bias, scale, carry[2 * j + m]))
            return tuple(out)
        fin = lax.fori_loop(0, nc, att_body, tuple(_flash_init(tq, 2 * DH_B) for _ in range(4)))
        for j, cols in enumerate(heads):
            o_ref[0, :, cols] = fin[2 * j][2] / fin[2 * j][1] - lam * (fin[2 * j + 1][2] / fin[2 * j + 1][1])


def _diff_prompt_pallas(q, k, v, lam, tq=128, kc=256):
    b, t, w = q.shape
    bf16 = jnp.bfloat16
    kern = functools.partial(_diff_prompt_kernel, tq=tq, kc=kc, scale=DH_B ** -0.5)
    blk = lambda bi, i: (bi, i, 0)
    full = lambda bi, i: (bi, 0, 0)
    return pl.pallas_call(
        kern,
        grid=(b, t // tq),
        in_specs=[pl.BlockSpec(memory_space=pltpu.SMEM),
                  pl.BlockSpec((1, tq, w), blk), pl.BlockSpec((1, t, w), full), pl.BlockSpec((1, t, w), full)],
        out_specs=pl.BlockSpec((1, tq, w), blk),
        out_shape=jax.ShapeDtypeStruct((b, t, w), jnp.float32),
        compiler_params=pltpu.CompilerParams(
            dimension_semantics=("arbitrary", "arbitrary"), vmem_limit_bytes=VMEM_LIMIT),
        name="diff_prompt",
    )(lam.reshape(1, 1).astype(jnp.float32), q.astype(bf16), k.astype(bf16), v.astype(bf16))


PPS = 8
TPAD = 8
NEG_INF_KEY = INT_MIN + 0x7FFFFF


def _float_order_key(x):
    bits = pltpu.bitcast(x, jnp.int32)
    bits = jnp.where(bits == INT_MIN, 0, bits)
    return jnp.where(bits < 0, bits ^ 0x7FFFFFFF, bits)


def _page_spec(l, r, rows, n_pages):
    def index(b, j, pt):
        return (l, pt[b * n_pages + j * PPS + r], 0, 0)
    return pl.BlockSpec((1, 1, rows, PAGE_SIZE), index)


def _rows_view(cache):
    return cache.reshape(cache.shape[0], cache.shape[1], cache.shape[2] * cache.shape[3], cache.shape[4])


def _online_softmax_step(state, s_list, pv):
    m_old, l, acc = state
    m_new = m_old
    for s in s_list:
        m_new = jnp.maximum(m_new, jnp.max(s, axis=1, keepdims=True))
    alpha = jnp.exp(m_old - m_new)
    p_list = [jnp.exp(s - m_new) for s in s_list]
    l = alpha * l
    for p in p_list:
        l = l + jnp.sum(p, axis=1, keepdims=True)
    return m_new, l, alpha * acc + pv(p_list)


def _online_softmax_update(m_ref, l_ref, acc_ref, s_list, pv):
    m_ref[...], l_ref[...], acc_ref[...] = _online_softmax_step((m_ref[...], l_ref[...], acc_ref[...]), s_list, pv)


def _sample_score_kernel(pt_ref, qh_ref, ql_ref, w_ref, kn_ref, *rest):
    ki_refs, (o_ref, on_ref) = rest[:PPS], rest[PPS:]
    f32 = jnp.float32
    qh, ql, w = qh_ref[0], ql_ref[0], w_ref[0]

    def scores(kt):
        kh, kl = _split3(kt)
        s = (jnp.dot(qh, kh, preferred_element_type=f32) + jnp.dot(qh, kl, preferred_element_type=f32)
             + jnp.dot(ql, kh, preferred_element_type=f32))
        r = w * jnp.maximum(s, 0.0)
        out = r[0:TPAD]
        for h in range(1, H_I):
            out = out + r[h * TPAD:(h + 1) * TPAD]
        return out
    for r in range(PPS):
        o_ref[0, :, r * PAGE_SIZE:(r + 1) * PAGE_SIZE] = scores(ki_refs[r][0, 0])

    @pl.when(pl.program_id(1) == 0)
    def _():
        t = lax.broadcasted_iota(jnp.int32, (TPAD, PAGE_SIZE), 0)
        j = lax.broadcasted_iota(jnp.int32, (TPAD, PAGE_SIZE), 1)
        on_ref[0] = jnp.where(j <= t, scores(kn_ref[0]), -jnp.inf)


def _sample_scores_pallas(l, qi, ki, wi, cache_ki, page_table):
    b, t = qi.shape[:2]
    n_pages = page_table.shape[1]
    qpad = jnp.zeros((b, H_I, TPAD, D_I), jnp.float32).at[:, :, :t].set(qi.transpose(0, 2, 1, 3))
    qh, ql = _split3(qpad.reshape(b, H_I * TPAD, D_I))
    wcol = jnp.zeros((b, H_I, TPAD), jnp.float32).at[:, :, :t].set(wi.transpose(0, 2, 1))
    wcol = (wcol * (H_I ** -0.5 * D_I ** -0.5)).reshape(b, H_I * TPAD, 1)
    knew = jnp.zeros((b, D_I, PAGE_SIZE), jnp.float32).at[:, :, :t].set(ki.transpose(0, 2, 1))
    row = lambda bi, j, pt: (bi, 0, 0)
    grid_spec = pltpu.PrefetchScalarGridSpec(
        num_scalar_prefetch=1, grid=(b, n_pages // PPS),
        in_specs=[pl.BlockSpec((1, H_I * TPAD, D_I), row), pl.BlockSpec((1, H_I * TPAD, D_I), row),
                  pl.BlockSpec((1, H_I * TPAD, 1), row), pl.BlockSpec((1, D_I, PAGE_SIZE), row)]
                 + [_page_spec(l, r, D_I, n_pages) for r in range(PPS)],
        out_specs=[pl.BlockSpec((1, TPAD, PPS * PAGE_SIZE), lambda bi, j, pt: (bi, 0, j)),
                   pl.BlockSpec((1, TPAD, PAGE_SIZE), row)])
    return pl.pallas_call(
        _sample_score_kernel, grid_spec=grid_spec,
        out_shape=[jax.ShapeDtypeStruct((b, TPAD, n_pages * PAGE_SIZE), jnp.float32),
                   jax.ShapeDtypeStruct((b, TPAD, PAGE_SIZE), jnp.float32)],
        compiler_params=pltpu.CompilerParams(dimension_semantics=("arbitrary", "arbitrary")),
        name="dsa_sample_scores",
    )(page_table.reshape(-1), qh, ql, wcol, knew, *([cache_ki.transpose(0, 1, 3, 2)] * PPS))


def _topk_bias_kernel(sc_ref, tri_ref, bias_ref, key_ref, *, n_sel):
    f32 = jnp.float32
    nc, rows, kc = sc_ref.shape

    def key_body(c, carry):
        key_ref[c] = _float_order_key(sc_ref[c])
        return carry
    lax.fori_loop(0, nc, key_body, 0)

    def count(pred, ref_val):
        def body(c, part):
            hit = jnp.where(pred(key_ref[c], ref_val), 1.0, 0.0)
            for j in range(kc // LANES):
                part = part + hit[:, j * LANES:(j + 1) * LANES]
            return part
        part = lax.fori_loop(0, nc, body, jnp.zeros((rows, LANES), f32))
        return jnp.sum(part, axis=1, keepdims=True)

    def bit_body(it, tau):
        cand = tau ^ jnp.left_shift(jnp.int32(1), 31 - it)
        return jnp.where(count(lambda a, b: a >= b, cand) >= n_sel, cand, tau)
    tau = lax.fori_loop(0, 32, bit_body, jnp.full((rows, 1), INT_MIN, jnp.int32))
    need = n_sel - count(lambda a, b: a > b, tau)

    def bias_body(c, seen):
        key = key_ref[c]
        tie = key == tau
        tie_f = jnp.where(tie, 1.0, 0.0)
        before = seen + jnp.dot(tie_f.astype(jnp.bfloat16), tri_ref[...], preferred_element_type=f32)
        sel = ((key > tau) | (tie & (before < need))) & (key > NEG_INF_KEY)
        bias_ref[c] = jnp.where(sel, 0.0, NEG_BIG)
        return seen + jnp.sum(tie_f, axis=1, keepdims=True)
    lax.fori_loop(0, nc, bias_body, jnp.zeros((rows, 1), f32))


def _topk_bias_pallas(scores, n_sel, kc=256):
    rows, n = scores.shape
    nc = n // kc
    tri = jnp.asarray(np.triu(np.ones((kc, kc), np.float32), 1), jnp.bfloat16)
    bias = pl.pallas_call(
        functools.partial(_topk_bias_kernel, n_sel=n_sel),
        out_shape=jax.ShapeDtypeStruct((nc, rows, kc), jnp.float32),
        scratch_shapes=[pltpu.VMEM((nc, rows, kc), jnp.int32)],
        compiler_params=pltpu.CompilerParams(vmem_limit_bytes=VMEM_LIMIT),
        name="topk_bias",
    )(scores.reshape(rows, nc, kc).transpose(1, 0, 2), tri)
    return bias.transpose(1, 0, 2).reshape(rows, n)


def _paged_attn_kernel(pt_ref, q_ref, kn_ref, vn_ref, bn_ref, *rest, scale, has_bias):
    f32 = jnp.float32
    bf16 = jnp.bfloat16
    if has_bias:
        bp_ref, rest = rest[0], rest[1:]
    k_refs, v_refs = rest[:PPS], rest[PPS:2 * PPS]
    o_ref, m_ref, l_ref, acc_ref = rest[2 * PPS:]
    j = pl.program_id(1)

    @pl.when(j == 0)
    def _():
        m_ref[...] = jnp.full(m_ref.shape, NEG_BIG, f32)
        l_ref[...] = jnp.zeros(l_ref.shape, f32)
        acc_ref[...] = jnp.zeros(acc_ref.shape, f32)

    q = q_ref[0]

    def update(s_list, v_list):
        m_old = m_ref[...]
        m_new = m_old
        for s in s_list:
            m_new = jnp.maximum(m_new, jnp.max(s, axis=1, keepdims=True))
        alpha = jnp.exp(m_old - m_new)
        l = alpha * l_ref[...]
        acc = alpha * acc_ref[...]
        for s, v in zip(s_list, v_list):
            p = jnp.exp(s - m_new)
            l = l + jnp.sum(p, axis=1, keepdims=True)
            acc = acc + jnp.dot(p.astype(bf16), v, preferred_element_type=f32)
        m_ref[...] = m_new
        l_ref[...] = l
        acc_ref[...] = acc

    s_list, v_list = [], []
    for r in range(PPS):
        s = lax.dot_general(q, k_refs[r][0, 0].astype(bf16), _NT, preferred_element_type=f32) * scale
        if has_bias:
            s = s + bp_ref[0, :, r * PAGE_SIZE:(r + 1) * PAGE_SIZE]
        s_list.append(s)
        v_list.append(v_refs[r][0, 0].astype(bf16))
    update(s_list, v_list)

    @pl.when(j == pl.num_programs(1) - 1)
    def _():
        s = lax.dot_general(q, kn_ref[0], _NT, preferred_element_type=f32) * scale + bn_ref[0]
        update([s], [vn_ref[0]])
        o_ref[0] = acc_ref[...] / l_ref[...]


def _paged_attn_pallas(l, qbd, k_new, v_new, bias_new, bias_past, cache_k, cache_v, page_table, scale):
    b, rws, w = qbd.shape
    t = k_new.shape[1]
    n_pages = page_table.shape[1]
    bf16 = jnp.bfloat16
    kn = jnp.zeros((b, TPAD, w), bf16).at[:, :t].set(k_new.astype(bf16))
    vn = jnp.zeros((b, TPAD, w), bf16).at[:, :t].set(v_new.astype(bf16))
    bn = jnp.full((b, rws, TPAD), NEG_BIG, jnp.float32).at[:, :, :t].set(bias_new)
    row = lambda bi, j, pt: (bi, 0, 0)
    in_specs = [pl.BlockSpec((1, rws, w), row), pl.BlockSpec((1, TPAD, w), row), pl.BlockSpec((1, TPAD, w), row),
                pl.BlockSpec((1, rws, TPAD), row)]
    args = [qbd.astype(bf16), kn, vn, bn]
    if bias_past is not None:
        in_specs.append(pl.BlockSpec((1, rws, PPS * PAGE_SIZE), lambda bi, j, pt: (bi, 0, j)))
        args.append(bias_past)
    in_specs += [_page_spec(l, r, w, n_pages) for r in range(PPS)] * 2
    args += [cache_k] * PPS + [cache_v] * PPS
    grid_spec = pltpu.PrefetchScalarGridSpec(
        num_scalar_prefetch=1, grid=(b, n_pages // PPS), in_specs=in_specs,
        out_specs=pl.BlockSpec((1, rws, w), row),
        scratch_shapes=[pltpu.VMEM((rws, 1), jnp.float32), pltpu.VMEM((rws, 1), jnp.float32),
                        pltpu.VMEM((rws, w), jnp.float32)])
    return pl.pallas_call(
        functools.partial(_paged_attn_kernel, scale=scale, has_bias=bias_past is not None),
        grid_spec=grid_spec,
        out_shape=jax.ShapeDtypeStruct((b, rws, w), jnp.float32),
        compiler_params=pltpu.CompilerParams(
            dimension_semantics=("arbitrary", "arbitrary"), vmem_limit_bytes=VMEM_LIMIT),
        name="paged_attn",
    )(page_table.reshape(-1), *args)


def _head_rows(x, n_blocks):
    b, t, w = x.shape
    blk = jnp.arange(w) // (w // n_blocks)
    keep = blk[None, :] == jnp.arange(n_blocks)[:, None]
    return jnp.where(keep[None, :, None, :], x[:, None, :, :], 0.0).reshape(b, n_blocks * t, w)


def _dsa_sample_pallas(l, q, k, v, qi, ki, wi_raw, cache_ki, cache_k, cache_v, page_table):
    b, t, h, dh = q.shape
    p_len = page_table.shape[1] * PAGE_SIZE
    n_sel = min(TOPK_MAX, (p_len + t) // 4)
    s_past, s_new = _sample_scores_pallas(l, qi, ki, wi_raw, cache_ki, page_table)
    pad = jnp.full((b, t, PAGE_SIZE), -jnp.inf, jnp.float32)
    scores = jnp.concatenate([s_past[:, :t], s_new[:, :t], pad], -1).reshape(b * t, p_len + 2 * PAGE_SIZE)
    bias = _topk_bias_pallas(scores, n_sel).reshape(b, t, -1)
    tile = lambda z: jnp.tile(z, (1, h, 1))
    w = h * dh
    res = _paged_attn_pallas(l, _head_rows(q.reshape(b, t, w), h), k.reshape(b, t, w), v.reshape(b, t, w),
                             tile(bias[:, :, p_len:p_len + t]), tile(bias[:, :, :p_len]),
                             cache_k.reshape(cache_k.shape[:3] + (w,)), cache_v.reshape(cache_v.shape[:3] + (w,)),
                             page_table, dh ** -0.5)
    res = res.reshape(b, h, t, h, dh)
    return jnp.stack([res[:, i, :, i] for i in range(h)], 2)


def _diff_sample_pallas(l, q, k, v, cache_k, cache_v, page_table, lam):
    b, t, h = q.shape[:3]
    w = h * 2 * DH_B
    causal = jnp.where(jnp.arange(t)[None, :] <= jnp.arange(t)[:, None], 0.0, NEG_BIG)
    bias_new = jnp.broadcast_to(jnp.tile(causal, (2 * h, 1))[None], (b, 2 * h * t, t))
    res = _paged_attn_pallas(l, _head_rows(q.reshape(b, t, w), 2 * h), k.reshape(b, t, w), v.reshape(b, t, w),
                             bias_new, None,
                             cache_k.reshape(cache_k.shape[:3] + (w,)), cache_v.reshape(cache_v.shape[:3] + (w,)),
                             page_table, DH_B ** -0.5)
    res = res.reshape(b, h, 2, t, h, 2 * DH_B)
    return jnp.stack([res[:, i, 0, :, i] - lam * res[:, i, 1, :, i] for i in range(h)], 2)


def _dsa_sample_attn_kernel(pt_ref, q_ref, kn_ref, vn_ref, bn_ref, bp_ref, *rest, scale):
    f32, bf16 = jnp.float32, jnp.bfloat16
    k_refs, v_refs = rest[:PPS], rest[PPS:2 * PPS]
    o_ref, m_ref, l_ref, acc_ref = rest[2 * PPS:]
    j = pl.program_id(1)
    wide = PAGE_SIZE * H_D

    @pl.when(j == 0)
    def _():
        m_ref[...] = jnp.full(m_ref.shape, NEG_BIG, f32)
        l_ref[...] = jnp.zeros(l_ref.shape, f32)
        acc_ref[...] = jnp.zeros(acc_ref.shape, f32)

    q = q_ref[0]
    s_list = [lax.dot_general(q, k_refs[r][0, 0].astype(bf16), _NT, preferred_element_type=f32) * scale
              + bp_ref[0, :, r * wide:(r + 1) * wide] for r in range(PPS)]

    def pv(p_list):
        out = jnp.dot(p_list[0].astype(bf16), v_refs[0][0, 0].astype(bf16), preferred_element_type=f32)
        for r in range(1, PPS):
            out = out + jnp.dot(p_list[r].astype(bf16), v_refs[r][0, 0].astype(bf16), preferred_element_type=f32)
        return out
    _online_softmax_update(m_ref, l_ref, acc_ref, s_list, pv)

    @pl.when(j == pl.num_programs(1) - 1)
    def _():
        s = lax.dot_general(q, kn_ref[0], _NT, preferred_element_type=f32) * scale + bn_ref[0]
        _online_softmax_update(m_ref, l_ref, acc_ref, [s],
                               lambda p: jnp.dot(p[0].astype(bf16), vn_ref[0], preferred_element_type=f32))
        o_ref[0] = acc_ref[...] / l_ref[...]


def _pad_tokens(x, fill=0.0):
    b, t = x.shape[:2]
    return jnp.full((b, TPAD) + x.shape[2:], fill, x.dtype).at[:, :t].set(x)


def _dsa_sample_pallas(l, q, k, v, qi, ki, wi_raw, cache_ki, cache_k, cache_v, page_table):
    b, t, h, dh = q.shape
    n_pages = page_table.shape[1]
    p_len = n_pages * PAGE_SIZE
    n_sel = min(TOPK_MAX, (p_len + t) // 4)
    bf16 = jnp.bfloat16
    s_past, s_new = _sample_scores_pallas(l, qi, ki, wi_raw, cache_ki, page_table)
    pad = jnp.full((b, t, PAGE_SIZE), -jnp.inf, jnp.float32)
    scores = jnp.concatenate([s_past[:, :t], s_new[:, :t], pad], -1).reshape(b * t, p_len + 2 * PAGE_SIZE)
    bias = _topk_bias_pallas(scores, n_sel).reshape(b, t, -1)
    rows = h * t
    same_head = jnp.arange(h)[:, None] == jnp.arange(h)[None, :]

    def per_head_bias(bz):
        full = jnp.where(same_head[None, :, None, None, :], bz[:, None, :, :, None], NEG_BIG)
        return full.reshape(b, rows, bz.shape[-1] * h)
    new_rows = TPAD * h
    bias_new = per_head_bias(jnp.full((b, t, TPAD), NEG_BIG, jnp.float32).at[:, :, :t].set(bias[:, :, p_len:p_len + t]))
    key_rows = lambda z: jnp.zeros((b, new_rows, dh), bf16).at[:, :t * h].set(z.reshape(b, t * h, dh).astype(bf16))
    row3 = lambda bi, j, pt: (bi, 0, 0)
    grid_spec = pltpu.PrefetchScalarGridSpec(
        num_scalar_prefetch=1, grid=(b, n_pages // PPS),
        in_specs=[pl.BlockSpec((1, rows, dh), row3), pl.BlockSpec((1, new_rows, dh), row3),
                  pl.BlockSpec((1, new_rows, dh), row3), pl.BlockSpec((1, rows, new_rows), row3),
                  pl.BlockSpec((1, rows, PPS * PAGE_SIZE * h), lambda bi, j, pt: (bi, 0, j))]
                 + [_page_spec(l, r, PAGE_SIZE * h, n_pages) for r in range(PPS)] * 2,
        out_specs=pl.BlockSpec((1, rows, dh), row3),
        scratch_shapes=[pltpu.VMEM((rows, 1), jnp.float32), pltpu.VMEM((rows, 1), jnp.float32),
                        pltpu.VMEM((rows, dh), jnp.float32)])
    res = pl.pallas_call(
        functools.partial(_dsa_sample_attn_kernel, scale=dh ** -0.5), grid_spec=grid_spec,
        out_shape=jax.ShapeDtypeStruct((b, rows, dh), jnp.float32),
        compiler_params=pltpu.CompilerParams(
            dimension_semantics=("arbitrary", "arbitrary"), vmem_limit_bytes=VMEM_LIMIT),
        name="dsa_sample_attn",
    )(page_table.reshape(-1), q.transpose(0, 2, 1, 3).reshape(b, rows, dh).astype(bf16), key_rows(k), key_rows(v),
      bias_new, per_head_bias(bias[:, :, :p_len]), *([_rows_view(cache_k)] * PPS), *([_rows_view(cache_v)] * PPS))
    return res.reshape(b, h, t, dh).transpose(0, 2, 1, 3)


def _diff_sample_attn_kernel(pt_ref, q_ref, knt_ref, vn_ref, bn_ref, *rest, scale):
    f32, bf16 = jnp.float32, jnp.bfloat16
    kt_refs, v_refs = rest[:PPS], rest[PPS:2 * PPS]
    o_ref, m_ref, l_ref, acc_ref = rest[2 * PPS:]
    j = pl.program_id(1)
    rph = 2 * TPAD // 2

    @pl.when(j == 0)
    def _():
        m_ref[...] = jnp.full(m_ref.shape, NEG_BIG, f32)
        l_ref[...] = jnp.zeros(l_ref.shape, f32)
        acc_ref[...] = jnp.zeros(acc_ref.shape, f32)

    q = q_ref[0]
    s_list = [jnp.dot(q, kt_refs[r][0, 0].astype(bf16), preferred_element_type=f32) * scale for r in range(PPS)]

    def pv(p_list):
        outs = []
        for h in range(H_B):
            head = pl.ds(h, PAGE_SIZE, stride=H_B)
            rows = slice(h * rph, (h + 1) * rph)
            out = jnp.dot(p_list[0][rows].astype(bf16), v_refs[0][0, 0, head, :].astype(bf16), preferred_element_type=f32)
            for r in range(1, PPS):
                out = out + jnp.dot(p_list[r][rows].astype(bf16), v_refs[r][0, 0, head, :].astype(bf16),
                                    preferred_element_type=f32)
            outs.append(out)
        return jnp.concatenate(outs, axis=0)
    _online_softmax_update(m_ref, l_ref, acc_ref, s_list, pv)

    @pl.when(j == pl.num_programs(1) - 1)
    def _():
        s = jnp.dot(q, knt_ref[0], preferred_element_type=f32) * scale + bn_ref[0]

        def pv_new(p):
            return jnp.concatenate([jnp.dot(p[0][h * rph:(h + 1) * rph].astype(bf16), vn_ref[0, h],
                                            preferred_element_type=f32) for h in range(H_B)], axis=0)
        _online_softmax_update(m_ref, l_ref, acc_ref, [s], pv_new)
        o_ref[0] = acc_ref[...] / l_ref[...]


def _diff_sample_pallas(l, q, k, v, cache_k, cache_v, page_table, lam):
    b, t, h = q.shape[:3]
    n_pages = page_table.shape[1]
    w = h * 2 * DH_B
    dv = 2 * DH_B
    tp = TPAD // 2
    rows = 2 * h * tp
    bf16 = jnp.bfloat16
    blk = jnp.arange(w) // DH_B
    qpad = jnp.zeros((b, tp, w), jnp.float32).at[:, :t].set(q.reshape(b, t, w))
    qbd = jnp.where((blk[None, :] == jnp.arange(2 * h)[:, None])[None, :, None, :], qpad[:, None], 0.0)
    qbd = qbd.reshape(b, rows, w).astype(bf16)
    knt = jnp.zeros((b, w, TPAD), jnp.float32).at[:, :, :t].set(k.reshape(b, t, w).transpose(0, 2, 1)).astype(bf16)
    vn = _pad_tokens(v).transpose(0, 2, 1, 3).astype(bf16)
    causal = jnp.where(jnp.arange(TPAD)[None, :] <= jnp.arange(tp)[:, None], 0.0, NEG_BIG)
    causal = jnp.where(jnp.arange(TPAD)[None, :] < t, causal, NEG_BIG)
    bias_new = jnp.broadcast_to(jnp.tile(causal, (2 * h, 1))[None], (b, rows, TPAD))
    kt_view = cache_k.transpose(0, 1, 3, 4, 5, 2).reshape(cache_k.shape[0], cache_k.shape[1], w, PAGE_SIZE)
    row3 = lambda bi, j, pt: (bi, 0, 0)
    grid_spec = pltpu.PrefetchScalarGridSpec(
        num_scalar_prefetch=1, grid=(b, n_pages // PPS),
        in_specs=[pl.BlockSpec((1, rows, w), row3), pl.BlockSpec((1, w, TPAD), row3),
                  pl.BlockSpec((1, h, TPAD, dv), lambda bi, j, pt: (bi, 0, 0, 0)), pl.BlockSpec((1, rows, TPAD), row3)]
                 + [_page_spec(l, r, w, n_pages) for r in range(PPS)]
                 + [_page_spec(l, r, PAGE_SIZE * h, n_pages) for r in range(PPS)],
        out_specs=pl.BlockSpec((1, rows, dv), row3),
        scratch_shapes=[pltpu.VMEM((rows, 1), jnp.float32), pltpu.VMEM((rows, 1), jnp.float32),
                        pltpu.VMEM((rows, dv), jnp.float32)])
    res = pl.pallas_call(
        functools.partial(_diff_sample_attn_kernel, scale=DH_B ** -0.5), grid_spec=grid_spec,
        out_shape=jax.ShapeDtypeStruct((b, rows, dv), jnp.float32),
        compiler_params=pltpu.CompilerParams(
            dimension_semantics=("arbitrary", "arbitrary"), vmem_limit_bytes=VMEM_LIMIT),
        name="diff_sample_attn",
    )(page_table.reshape(-1), qbd, knt, vn, bias_new, *([kt_view] * PPS), *([_rows_view(cache_v)] * PPS))
    res = res.reshape(b, h, 2, tp, dv)[:, :, :, :t]
    return (res[:, :, 0] - lam * res[:, :, 1]).transpose(0, 2, 1, 3)


_TN = (((0,), (0,)), ((), ()))


def _silu(x):
    return x / (1.0 + jnp.exp(-x))


def _log_sigmoid(x):
    return jnp.minimum(x, 0.0) - jnp.log(1.0 + jnp.exp(-jnp.abs(x)))


def _head_rms_gate(o, g, z, d):
    outs = []
    for h in range(o.shape[1] // d):
        cols = slice(h * d, (h + 1) * d)
        oh = o[:, cols]
        outs.append(oh * lax.rsqrt(jnp.mean(oh * oh, axis=1, keepdims=True) + NORM_EPS))
    return jnp.concatenate(outs, axis=1) * g * _silu(z)


def _gla_kernel(q_ref, k_ref, v_ref, g_ref, wg_ref, bg_ref, z_ref, gn_ref, tril_ref, s0_ref,
                y_ref, sout_ref, st_ref, la_ref, o_ref, *, tb, t_valid):
    f32, bf16 = jnp.float32, jnp.bfloat16
    c = GLA_CHUNK

    @pl.when(pl.program_id(1) == 0)
    def _():
        st_ref[...] = s0_ref[0]

    gl = jnp.dot(g_ref[0].astype(bf16), wg_ref[...].astype(bf16), preferred_element_type=f32) + bg_ref[...]
    la = _log_sigmoid(gl) / GATE_TAU
    if t_valid < tb:
        la = jnp.where(lax.broadcasted_iota(jnp.int32, (tb, 1), 0) < t_valid, la, 0.0)
    la_ref[...] = la
    trow = lax.broadcasted_iota(jnp.int32, (c, 1), 0)
    lane_id = lax.broadcasted_iota(jnp.int32, (1, LANES), 1)

    def chunk(ci, carry):
        rows = pl.ds(pl.multiple_of(ci * c, c), c)
        cb = jnp.dot(tril_ref[...], la_ref[rows, :], precision=lax.Precision.HIGHEST, preferred_element_type=f32)
        c_last = cb[c - 1:c, :]
        qc = q_ref[0, rows, :] * (DK_A ** -0.5)
        kc = k_ref[0, rows, :]
        vc = v_ref[0, rows, :]
        qe = (qc * jnp.exp(cb)).astype(bf16)
        kdec = (kc * jnp.exp(c_last - cb)).astype(bf16)
        dec = jnp.exp(c_last)
        s_ts = [st_ref[h] for h in range(H_A)]
        intra = [[] for _ in range(H_A)]
        for pair in range(H_A // 2):
            slab = slice(pair * LANES, (pair + 1) * LANES)
            q_p, k_p, cb_p = qc[:, slab], kc[:, slab], cb[:, slab]
            for s in range(c):
                x = q_p * k_p[s:s + 1, :] * jnp.exp(cb_p - cb_p[s:s + 1, :])
                for half in range(2):
                    h = 2 * pair + half
                    col = jnp.sum(jnp.where((lane_id >= half * DK_A) & (lane_id < (half + 1) * DK_A), x, 0.0),
                                  axis=1, keepdims=True)
                    intra[h].append(jnp.where(trow >= s, col, 0.0) * vc[s:s + 1, h * DV_A:(h + 1) * DV_A])
        outs = []
        for h in range(H_A):
            kcols = slice(h * DK_A, (h + 1) * DK_A)
            parts = intra[h]
            while len(parts) > 1:
                parts = [parts[i] + parts[i + 1] for i in range(0, len(parts), 2)]
            outs.append(parts[0] + lax.dot_general(qe[:, kcols], s_ts[h].astype(bf16), _NT, preferred_element_type=f32))
            s_ts[h] = s_ts[h] * dec[:, kcols] + lax.dot_general(
                vc[:, h * DV_A:(h + 1) * DV_A].astype(bf16), kdec[:, kcols], _TN, preferred_element_type=f32)
        o_ref[rows, :] = jnp.concatenate(outs, axis=1)
        for h in range(H_A):
            st_ref[h] = s_ts[h]
        return carry
    lax.fori_loop(0, tb // c, chunk, 0)
    y_ref[0] = _head_rms_gate(o_ref[...], gn_ref[...], z_ref[0], DV_A)

    @pl.when(pl.program_id(1) == pl.num_programs(1) - 1)
    def _():
        sout_ref[0] = st_ref[...]


def _gla_pallas(q, k, v, g, z, w_gate, b_gate, g_norm, s0, tb):
    b, t, _ = q.shape
    tp = -(-t // tb) * tb
    pad = lambda x: x if tp == t else jnp.zeros((b, tp, x.shape[2]), x.dtype).at[:, :t].set(x)
    tril = jnp.asarray(np.tril(np.ones((GLA_CHUNK, GLA_CHUNK), np.float32)))
    blk = lambda w: pl.BlockSpec((1, tb, w), lambda bi, j: (bi, j, 0))
    const = lambda shape: pl.BlockSpec(shape, lambda bi, j: (0,) * len(shape))
    st_spec = pl.BlockSpec((1, H_A, DV_A, DK_A), lambda bi, j: (bi, 0, 0, 0))
    y, s_t = pl.pallas_call(
        functools.partial(_gla_kernel, tb=tb, t_valid=min(t, tb)),
        grid=(b, tp // tb),
        in_specs=[blk(H_A * DK_A), blk(H_A * DK_A), blk(W_BR), blk(GATE_RANK),
                  const((GATE_RANK, H_A * DK_A)), const((1, H_A * DK_A)), blk(W_BR), const((1, W_BR)),
                  const((GLA_CHUNK, GLA_CHUNK)), st_spec],
        out_specs=[blk(W_BR), st_spec],
        out_shape=[jax.ShapeDtypeStruct((b, tp, W_BR), jnp.float32),
                   jax.ShapeDtypeStruct((b, H_A, DV_A, DK_A), jnp.float32)],
        scratch_shapes=[pltpu.VMEM((H_A, DV_A, DK_A), jnp.float32), pltpu.VMEM((tb, H_A * DK_A), jnp.float32),
                        pltpu.VMEM((tb, W_BR), jnp.float32)],
        compiler_params=pltpu.CompilerParams(
            dimension_semantics=("arbitrary", "arbitrary"), vmem_limit_bytes=VMEM_LIMIT),
        name="gla",
    )(pad(q), pad(k), pad(v), pad(g), w_gate, b_gate.reshape(1, -1), pad(z), g_norm.reshape(1, -1), tril,
      s0.swapaxes(2, 3))
    return y[:, :t], s_t.swapaxes(2, 3)


CONV_PAD = 8


def _mlstm_kernel(cq_ref, ck_ref, cv_ref, co_ref, cz_ref, icol_ref, fcol_ref, irow_ref, frow_ref,
                  wq_ref, wk_ref, gn_ref, tril_ref, conv0_ref, c0_ref, n0_ref, m0_ref,
                  y_ref, conv_ref, c_ref, n_ref, m_ref, xq_ref, xk_ref, cm_ref, nv_ref, mv_ref, *, ch, t_valid):
    f32, bf16 = jnp.float32, jnp.bfloat16
    hi = lax.Precision.HIGHEST
    first = CONV_PAD - (CONV_W - 1)

    @pl.when(pl.program_id(1) == 0)
    def _():
        xq_ref[0:CONV_PAD, :] = conv0_ref[0, :, 0:W_BR]
        xk_ref[0:CONV_PAD, :] = conv0_ref[0, :, W_BR:2 * W_BR]
        cm_ref[...] = c0_ref[0]
        nv_ref[...] = n0_ref[0]
        mv_ref[...] = m0_ref[0]

    def conv(x_ref, new_ref, w_ref, cols):
        x_ref[CONV_PAD:CONV_PAD + ch, :] = new_ref[0]
        out = x_ref[first:first + ch, :] * w_ref[0:1, :]
        for j in range(1, CONV_W):
            out = out + x_ref[first + j:first + j + ch, :] * w_ref[j:j + 1, :]
        conv_ref[0, :, cols] = x_ref[first + t_valid:first + t_valid + CONV_W - 1, :]
        tail = x_ref[ch:ch + CONV_PAD, :]
        x_ref[0:CONV_PAD, :] = tail
        return _silu(out)
    qc = conv(xq_ref, cq_ref, wq_ref, slice(0, W_BR))
    kc = conv(xk_ref, ck_ref, wk_ref, slice(W_BR, 2 * W_BR)) * (DH_C ** -0.5)

    tril = tril_ref[...]
    fcum_c = jnp.dot(tril, _log_sigmoid(fcol_ref[0]), precision=hi, preferred_element_type=f32)
    fcum_r = lax.dot_general(_log_sigmoid(frow_ref[0]), tril, _NT, precision=hi, preferred_element_type=f32)
    i_c, i_r = icol_ref[0], irow_ref[0]
    causal = lax.broadcasted_iota(jnp.int32, (ch, ch), 1) <= lax.broadcasted_iota(jnp.int32, (ch, ch), 0)
    outs = []
    for h in range(H_C):
        cols = slice(h * DH_C, (h + 1) * DH_C)
        q_h, k_h, v_h = qc[:, cols], kc[:, cols], cv_ref[0, :, cols]
        cm, nv, m_prev = cm_ref[h], nv_ref[h:h + 1, :], mv_ref[h:h + 1, 0:1]
        fc, fr, ic, ir = fcum_c[:, h:h + 1], fcum_r[h:h + 1, :], i_c[:, h:h + 1], i_r[h:h + 1, :]
        dmat = jnp.where(causal, fc - fr + ir, NEG_BIG)
        inter = fc + m_prev
        m_t = jnp.maximum(inter, jnp.max(dmat, axis=1, keepdims=True))
        w_inter = jnp.exp(inter - m_t)
        qk = lax.dot_general(q_h.astype(bf16), k_h.astype(bf16), _NT, preferred_element_type=f32) * jnp.exp(dmat - m_t)
        num = (w_inter * jnp.dot(q_h.astype(bf16), cm.astype(bf16), preferred_element_type=f32)
               + jnp.dot(qk.astype(bf16), v_h.astype(bf16), preferred_element_type=f32))
        den = w_inter * jnp.sum(q_h * nv, axis=1, keepdims=True) + jnp.sum(qk, axis=1, keepdims=True)
        hv = num / jnp.maximum(jnp.abs(den), jnp.exp(-m_t))
        outs.append(hv / (1.0 + jnp.exp(-co_ref[0, :, cols])))
        f_last = fc[ch - 1:ch, :]
        m_new = jnp.maximum(f_last + m_prev, jnp.max(f_last - fr + ir, axis=1, keepdims=True))
        a = jnp.exp(f_last + m_prev - m_new)
        kw = k_h * jnp.exp(f_last - fc + ic - m_new)
        cm_ref[h] = a * cm + lax.dot_general(kw.astype(bf16), v_h.astype(bf16), _TN, preferred_element_type=f32)
        nv_ref[h:h + 1, :] = a * nv + jnp.sum(kw, axis=0, keepdims=True)
        mv_ref[h:h + 1, :] = jnp.broadcast_to(m_new, (1, LANES))
    y_ref[0] = _head_rms_gate(jnp.concatenate(outs, axis=1), gn_ref[...], cz_ref[0], DH_C)

    @pl.when(pl.program_id(1) == pl.num_programs(1) - 1)
    def _():
        c_ref[0] = cm_ref[...]
        n_ref[0] = nv_ref[...]
        m_ref[0] = mv_ref[...]


def _mlstm_pallas(cq, ck, cv, co, cz, i_pre, f_pre, w_conv, g_norm, conv0, c0, n0, m0, ch):
    b, t, _ = cq.shape
    tp = -(-t // ch) * ch
    f32 = jnp.float32
    pad = lambda x, fill=0.0: x if tp == t else jnp.full((b, tp, x.shape[2]), fill, x.dtype).at[:, :t].set(x)
    i_pre, f_pre = pad(i_pre, NEG_BIG), pad(f_pre, -NEG_BIG)
    rows8 = lambda x: jnp.zeros((b, 8, x.shape[2]), f32).at[:, :x.shape[1]].set(x)
    tril = jnp.asarray(np.tril(np.ones((ch, ch), np.float32)))
    conv_hist = jnp.zeros((b, CONV_PAD, 2 * W_BR), f32).at[:, CONV_PAD - (CONV_W - 1):].set(conv0)
    assert tp == t or tp == ch
    blk = lambda w: pl.BlockSpec((1, ch, w), lambda bi, j: (bi, j, 0))
    rowblk = pl.BlockSpec((1, 8, ch), lambda bi, j: (bi, 0, j))
    const = lambda shape: pl.BlockSpec(shape, lambda bi, j: (0,) * len(shape))
    per_b = lambda shape: pl.BlockSpec((1,) + shape, lambda bi, j: (bi,) + (0,) * len(shape))
    y, conv_s, c_f, n_f, m_f = pl.pallas_call(
        functools.partial(_mlstm_kernel, ch=ch, t_valid=t - (tp - ch)),
        grid=(b, tp // ch),
        in_specs=[blk(W_BR)] * 5 + [blk(H_C), blk(H_C), rowblk, rowblk,
                  const((CONV_W, W_BR)), const((CONV_W, W_BR)), const((1, W_BR)), const((ch, ch)),
                  per_b((CONV_PAD, 2 * W_BR)), per_b((H_C, DH_C, DH_C)), per_b((8, DH_C)), per_b((8, LANES))],
        out_specs=[blk(W_BR), per_b((CONV_W - 1, 2 * W_BR)), per_b((H_C, DH_C, DH_C)), per_b((8, DH_C)),
                   per_b((8, LANES))],
        out_shape=[jax.ShapeDtypeStruct((b, tp, W_BR), f32), jax.ShapeDtypeStruct((b, CONV_W - 1, 2 * W_BR), f32),
                   jax.ShapeDtypeStruct((b, H_C, DH_C, DH_C), f32), jax.ShapeDtypeStruct((b, 8, DH_C), f32),
                   jax.ShapeDtypeStruct((b, 8, LANES), f32)],
        scratch_shapes=[pltpu.VMEM((CONV_PAD + ch, W_BR), f32), pltpu.VMEM((CONV_PAD + ch, W_BR), f32),
                        pltpu.VMEM((H_C, DH_C, DH_C), f32), pltpu.VMEM((8, DH_C), f32), pltpu.VMEM((8, LANES), f32)],
        compiler_params=pltpu.CompilerParams(
            dimension_semantics=("arbitrary", "arbitrary"), vmem_limit_bytes=VMEM_LIMIT),
        name="mlstm",
    )(pad(cq), pad(ck), pad(cv), pad(co), pad(cz), i_pre, f_pre,
      rows8(i_pre.transpose(0, 2, 1)), rows8(f_pre.transpose(0, 2, 1)),
      w_conv[:, :W_BR], w_conv[:, W_BR:], g_norm.reshape(1, -1), tril, conv_hist, c0, rows8(n0),
      rows8(jnp.broadcast_to(m0[:, :, None], (b, H_C, LANES))))
    return y[:, :t], conv_s, c_f, n_f[:, :H_C], m_f[:, :H_C, 0]


def _merge_kernel(ys_ref, gate_ref, wb_ref, wo_ref, x_ref, g_ref, b_ref, o_ref, acc_ref):
    f32, bf16 = jnp.float32, jnp.bfloat16
    n = pl.program_id(1)
    proj = jnp.dot(ys_ref[0].astype(bf16), wb_ref[0], preferred_element_type=f32)
    term = proj / (1.0 + jnp.exp(-gate_ref[...]))

    @pl.when(n == 0)
    def _():
        acc_ref[...] = term

    @pl.when(n > 0)
    def _():
        acc_ref[...] += term

    @pl.when(n == N_BRANCH - 1)
    def _():
        out = jnp.dot(acc_ref[...].astype(bf16), wo_ref[...], preferred_element_type=f32)
        xf = DN_ALPHA * x_ref[...] + out
        xc = xf - jnp.mean(xf, axis=1, keepdims=True)
        var = jnp.mean(xc * xc, axis=1, keepdims=True)
        o_ref[...] = xc * lax.rsqrt(var + LN_EPS) * g_ref[...] + b_ref[...]


def _merge_pallas(ys, gate, w_branch, w_out, x, ln_g, ln_b, tm):
    m = x.shape[0]
    return pl.pallas_call(
        _merge_kernel,
        grid=(m // tm, N_BRANCH),
        in_specs=[pl.BlockSpec((1, tm, W_BR), lambda i, n: (n, i, 0)),
                  pl.BlockSpec((tm, D_MODEL), lambda i, n: (i, n)),
                  pl.BlockSpec((1, W_BR, D_MODEL), lambda i, n: (n, 0, 0)),
                  pl.BlockSpec((D_MODEL, D_MODEL), lambda i, n: (0, 0)),
                  pl.BlockSpec((tm, D_MODEL), lambda i, n: (i, 0)),
                  pl.BlockSpec((1, D_MODEL), lambda i, n: (0, 0)),
                  pl.BlockSpec((1, D_MODEL), lambda i, n: (0, 0))],
        out_specs=pl.BlockSpec((tm, D_MODEL), lambda i, n: (i, 0)),
        out_shape=jax.ShapeDtypeStruct((m, D_MODEL), jnp.float32),
        scratch_shapes=[pltpu.VMEM((tm, D_MODEL), jnp.float32)],
        compiler_params=pltpu.CompilerParams(
            dimension_semantics=("arbitrary", "arbitrary"), vmem_limit_bytes=VMEM_LIMIT),
        name="merge",
    )(ys, gate, w_branch, w_out, x, ln_g.reshape(1, -1), ln_b.reshape(1, -1))


def _split_in(u):
    parts = {}
    off = 0
    for name, width in IN_WIDTHS:
        parts[name] = u[..., off:off + width]
        off += width
    return parts


def _rope(x, pos):
    half = x.shape[-1] // 2
    inv = ROPE_THETA ** (-jnp.arange(half, dtype=jnp.float32) / half)
    ang = pos.astype(jnp.float32)[:, None] * inv[None, :]
    cos = jnp.cos(ang)[:, None, :]
    sin = jnp.sin(ang)[:, None, :]
    x1 = x[..., :half].astype(jnp.float32)
    x2 = x[..., half:].astype(jnp.float32)
    return jnp.concatenate([x1 * cos - x2 * sin, x1 * sin + x2 * cos], -1).astype(x.dtype)


def _head_rms(h, g):
    hf = h.astype(jnp.float32)
    hf = hf * lax.rsqrt(jnp.mean(hf * hf, -1, keepdims=True) + NORM_EPS)
    return (hf.reshape(h.shape[:-2] + (-1,)) * g).astype(h.dtype)


def _layernorm(x, g, b):
    xf = x.astype(jnp.float32)
    xc = xf - jnp.mean(xf, -1, keepdims=True)
    var = jnp.mean(xc * xc, -1, keepdims=True)
    return (xc * lax.rsqrt(var + LN_EPS) * g + b).astype(x.dtype)


def _causal_conv(xs, buf, w):
    t = xs.shape[1]
    full = jnp.concatenate([buf.astype(xs.dtype), xs], 1)
    out = full[:, 0:t] * w[0]
    for j in range(1, CONV_W):
        out = out + full[:, j:j + t] * w[j]
    return out, full[:, t:]


def _gla(q, k, v, log_a, s0):
    b, t, h, dk = q.shape
    dt = v.dtype
    c = math.gcd(t, GLA_CHUNK)
    n = t // c
    f32 = jnp.float32
    def chunks(z):
        return z.astype(f32).reshape(b, n, c, h, -1).transpose(1, 0, 3, 2, 4)
    mask = jnp.tril(jnp.ones((c, c), bool))
    def step(s, inp):
        qb, kb, vb, ab = inp
        cb = jnp.cumsum(ab, axis=2)
        o_inter = jnp.einsum('bhtk,bhkv->bhtv', qb * jnp.exp(cb), s)
        diff = cb[:, :, :, None, :] - cb[:, :, None, :, :]
        decay = jnp.exp(jnp.where(mask[:, :, None], diff, -jnp.inf))
        att = jnp.einsum('bhtk,bhsk,bhtsk->bhts', qb, kb, decay)
        o = o_inter + att @ vb
        c_last = cb[:, :, -1:, :]
        s_new = jnp.exp(c_last[:, :, 0, :])[..., None] * s + jnp.einsum('bhsk,bhsv->bhkv', kb * jnp.exp(c_last - cb), vb)
        return s_new, o
    s_fin, o = lax.scan(step, s0.astype(f32), (chunks(q * (dk ** -0.5)), chunks(k), chunks(v), chunks(log_a)))
    o = o.transpose(1, 0, 3, 2, 4).reshape(b, t, h, -1)
    return o.astype(dt), s_fin.astype(dt)


def _mlstm(q, k, v, i_pre, f_pre, c0, n0, m0):
    b, t, h, d = q.shape
    dt = v.dtype
    f32 = jnp.float32
    c = math.gcd(t, MLSTM_CHUNK)
    n = t // c
    def chunks(z):
        return z.astype(f32).reshape(b, n, c, h, -1).transpose(1, 0, 3, 2, 4)
    def gchunks(z):
        return z.astype(f32).reshape(b, n, c, h).transpose(1, 0, 3, 2)
    mask = jnp.tril(jnp.ones((c, c), bool))
    def step(carry, inp):
        cm, nv, m = carry
        qb, kb, vb, ib, fb = inp
        fcum = jnp.cumsum(jax.nn.log_sigmoid(fb), -1)
        dmat = jnp.where(mask, fcum[..., :, None] - fcum[..., None, :] + ib[..., None, :], -jnp.inf)
        inter = fcum + m[..., None]
        m_t = jnp.maximum(inter, jnp.max(dmat, -1))
        w_inter = jnp.exp(inter - m_t)
        qk = jnp.einsum('bhtd,bhsd->bhts', qb, kb) * jnp.exp(dmat - m_t[..., None])
        num = w_inter[..., None] * jnp.einsum('bhtd,bhde->bhte', qb, cm) + qk @ vb
        den = w_inter * jnp.einsum('bhtd,bhd->bht', qb, nv) + jnp.sum(qk, -1)
        hv = num / jnp.maximum(jnp.abs(den), jnp.exp(-m_t))[..., None]
        f_last = fcum[..., -1]
        g_s = f_last[..., None] - fcum + ib
        m_new = jnp.maximum(f_last + m, jnp.max(g_s, -1))
        a = jnp.exp(f_last + m - m_new)
        ws = jnp.exp(g_s - m_new[..., None])
        c_new = a[..., None, None] * cm + jnp.einsum('bhs,bhsd,bhse->bhde', ws, kb, vb)
        n_new = a[..., None] * nv + jnp.einsum('bhs,bhsd->bhd', ws, kb)
        return (c_new, n_new, m_new), hv
    (c_f, n_f, m_f), hs = lax.scan(step, (c0.astype(f32), n0.astype(f32), m0.astype(f32)),
                                   (chunks(q), chunks(k), chunks(v), gchunks(i_pre), gchunks(f_pre)))
    hs = hs.transpose(1, 0, 3, 2, 4).reshape(b, t, h, d)
    return hs.astype(dt), c_f.astype(dt), n_f.astype(dt), m_f.astype(dt)


def _diff_attn_prompt(q, k, v, lam):
    b, s, h, _, d = q.shape
    nb = s // Q_BLOCK
    qb = q.reshape(b, nb, Q_BLOCK, h, 2, d).swapaxes(0, 1)
    kpos = jnp.arange(s)
    def one_block(args):
        qi, i = args
        qpos = i * Q_BLOCK + jnp.arange(Q_BLOCK)
        sc = jnp.einsum('bthmd,bshmd->bhmts', qi, k).astype(jnp.float32) * (d ** -0.5)
        sc = jnp.where(kpos[None, :] <= qpos[:, None], sc, -jnp.inf)
        p = jax.nn.softmax(sc, -1)
        a = (p[:, :, 0] - lam * p[:, :, 1]).astype(v.dtype)
        return jnp.einsum('bhts,bshe->bthe', a, v)
    o = lax.map(one_block, (qb, jnp.arange(nb)))
    return o.swapaxes(0, 1).reshape(b, s, h, 2 * d)


def _diff_attn_sample(q, k, v, k_past, v_past, lam):
    t = q.shape[1]
    d = q.shape[-1]
    p_len = k_past.shape[1]
    causal = jnp.tril(jnp.ones((t, t), bool))
    s_p = jnp.einsum('bthmd,bshmd->bhmts', q, k_past).astype(jnp.float32)
    s_n = jnp.where(causal, jnp.einsum('bthmd,bshmd->bhmts', q, k).astype(jnp.float32), -jnp.inf)
    p = jax.nn.softmax(jnp.concatenate([s_p, s_n], -1) * (d ** -0.5), -1)
    a = (p[:, :, 0] - lam * p[:, :, 1]).astype(v.dtype)
    return jnp.einsum('bhts,bshe->bthe', a[..., :p_len], v_past) + jnp.einsum('bhts,bshe->bthe', a[..., p_len:], v)


def _index_scores(qi, ki, wi):
    sc = jnp.einsum('bthi,bsi->bths', qi, ki).astype(jnp.float32) * (D_I ** -0.5)
    return jnp.einsum('bths,bth->bts', jax.nn.relu(sc), wi.astype(jnp.float32))


def _dsa_prompt(q, k, v, qi, ki, wi):
    b, s, h, dh = q.shape
    n_sel = min(TOPK_MAX, s // 4)
    nb = s // Q_BLOCK
    def blk(z):
        return z.reshape((b, nb, Q_BLOCK) + z.shape[2:]).swapaxes(0, 1)
    kpos = jnp.arange(s)
    gather = jax.vmap(lambda rows, sel: rows[sel])
    def one_block(args):
        qb, qib, wib, i = args
        qpos = i * Q_BLOCK + jnp.arange(Q_BLOCK)
        score = jnp.where(kpos[None, :] <= qpos[:, None], _index_scores(qib, ki, wib), -jnp.inf)
        _, sel = lax.top_k(score, n_sel)
        kg = gather(k, sel)
        vg = gather(v, sel)
        att = jnp.einsum('bthd,btkhd->bhtk', qb, kg).astype(jnp.float32) * (dh ** -0.5)
        valid = sel <= qpos[None, :, None]
        p = jax.nn.softmax(jnp.where(valid[:, None], att, -jnp.inf), -1)
        return jnp.einsum('bhtk,btkhd->bthd', p.astype(v.dtype), vg)
    o = lax.map(one_block, (blk(q), blk(qi), blk(wi), jnp.arange(nb)))
    return o.swapaxes(0, 1).reshape(b, s, h, dh)


def _dsa_sample(q, k, v, qi, ki, wi, ki_past, pool_k, pool_v, page_table):
    b, t, h, dh = q.shape
    p_len = ki_past.shape[1]
    n_sel = min(TOPK_MAX, (p_len + t) // 4)
    causal = jnp.tril(jnp.ones((t, t), bool))
    score = jnp.concatenate([_index_scores(qi, ki_past, wi),
                             jnp.where(causal, _index_scores(qi, ki, wi), -jnp.inf)], -1)
    _, sel = lax.top_k(score, n_sel)
    ps = jnp.minimum(sel, p_len - 1)
    phys = jnp.take_along_axis(page_table, (ps // PAGE_SIZE).reshape(b, -1), axis=1).reshape(sel.shape)
    flat = phys * PAGE_SIZE + ps % PAGE_SIZE
    ns = jnp.clip(sel - p_len, 0, t - 1)
    gather = jax.vmap(lambda rows, idx: rows[idx])
    is_past = (sel < p_len)[..., None, None]
    kg = jnp.where(is_past, pool_k.reshape((-1, h, dh))[flat], gather(k, ns))
    vg = jnp.where(is_past, pool_v.reshape((-1, h, dh))[flat], gather(v, ns))
    att = jnp.einsum('bthd,btkhd->bhtk', q, kg).astype(jnp.float32) * (dh ** -0.5)
    valid = sel <= (p_len + jnp.arange(t))[None, :, None]
    p = jax.nn.softmax(jnp.where(valid[:, None], att, -jnp.inf), -1)
    return jnp.einsum('bhtk,btkhd->bthd', p.astype(v.dtype), vg)


def _sublayer(x, pos, l, w_in_p, w_a_gate, b_a_gate, g_a, lam_qk, g_b, b_c_if, w_c_conv, g_c,
              w_branch, w_out, ln_g, ln_b, rec, paged):
    b, t, _ = x.shape
    f32 = jnp.float32
    a_s0, c_c0, c_n0, c_m0, c_conv0 = rec
    m = b * t
    w_main, w_gate = w_in_p
    xb = x.reshape(m, D_MODEL).astype(jnp.bfloat16)
    u = _split_in(_matmul(xb, w_main[l], min(m, 512), N_MAIN_PAD // 4)[:, :N_MAIN].reshape(b, t, N_MAIN))
    gate = _matmul(xb, w_gate[l], min(m, 512), D_MODEL)
    long_seq = t >= 512
    ya, a_s = _gla_pallas(u['a_q'], u['a_k'], u['a_v'], u['a_g'], u['a_z'], w_a_gate[l], b_a_gate[l], g_a[l], a_s0,
                          512 if long_seq else GLA_CHUNK)
    qb = _rope(u['b_q'].reshape(b, t, 2 * H_B, DH_B), pos).reshape(b, t, H_B, 2, DH_B)
    kb = _rope(u['b_k'].reshape(b, t, 2 * H_B, DH_B), pos).reshape(b, t, H_B, 2, DH_B)
    vb = u['b_v'].reshape(b, t, H_B, 2 * DH_B)
    lam_init = 0.8 - 0.6 * math.exp(-0.3 * l)
    lq = lam_qk[l].astype(f32)
    lam = jnp.exp(jnp.sum(lq[0] * lq[1])) - jnp.exp(jnp.sum(lq[2] * lq[3])) + lam_init
    if paged is None:
        ob = _diff_prompt_pallas(qb.reshape(b, t, W_BR), kb.reshape(b, t, W_BR), vb.reshape(b, t, W_BR),
                                 lam).reshape(b, t, H_B, 2 * DH_B)
    else:
        page_table, pb_k, pb_v, pd_k, pd_v, pd_ki = paged
        ob = _diff_sample_pallas(l, qb, kb, vb, pb_k, pb_v, page_table, lam)
    yb = _head_rms(ob, g_b[l]) * (1.0 - lam_init) * jax.nn.silu(u['b_z'])
    i_pre = u['c_i'].astype(f32) + b_c_if[l, 0]
    f_pre = u['c_f'].astype(f32) + b_c_if[l, 1]
    yc, c_conv, c_c, c_n, c_m = _mlstm_pallas(u['c_q'], u['c_k'], u['c_v'], u['c_o'], u['c_z'], i_pre, f_pre,
                                              w_c_conv[l], g_c[l], c_conv0, c_c0, c_n0, c_m0,
                                              128 if long_seq else TPAD)
    qd = _rope(u['d_q'].reshape(b, t, H_D, DH_D), pos)
    kd = _rope(u['d_k'].reshape(b, t, H_D, DH_D), pos)
    vd = u['d_v'].reshape(b, t, H_D, DH_D)
    qi = _rope(u['d_qi'].reshape(b, t, H_I, D_I), pos)
    ki = _rope(u['d_ki'].reshape(b, t, 1, D_I), pos)[:, :, 0]
    wi = u['d_w'] * (H_I ** -0.5)
    if paged is None:
        od = _dsa_prompt_pallas(qd.reshape(b, t, W_BR), kd.reshape(b, t, W_BR), vd.reshape(b, t, W_BR),
                                qi.reshape(b, t, H_I * D_I), ki, u['d_w'], min(TOPK_MAX, t // 4))
    else:
        od = _dsa_sample_pallas(l, qd, kd, vd, qi, ki, u['d_w'], pd_ki, pd_k, pd_v, page_table)
    yd = od.reshape(b, t, W_BR) * jax.nn.silu(u['d_z'])
    ys = jnp.stack([ya, yb, yc, yd], 0).reshape(N_BRANCH, m, W_BR)
    new_x = _merge_pallas(ys, gate, w_branch[l], w_out[l], x.reshape(m, D_MODEL), ln_g[l], ln_b[l], min(m, 256))
    return new_x.reshape(b, t, D_MODEL), (kb, vb, kd, vd, ki, a_s, c_c, c_n, c_m, c_conv)


def kernel(x_prompt, x_sample, cache_b_k, cache_b_v, cache_d_k, cache_d_v, cache_d_ki,
           state_a_s, state_c_c, state_c_n, state_c_m, state_c_conv, page_table,
           w_in, w_a_gate, b_a_gate, g_a, lam_qk, g_b, b_c_if, w_c_conv, g_c,
           w_branch, w_out, ln_g, ln_b):
    bp, tp, _ = x_prompt.shape
    bd, td, _ = x_sample.shape
    past = page_table.shape[1] * PAGE_SIZE
    pos_p = jnp.arange(tp)
    pos_s = past + jnp.arange(td)
    dt = x_prompt.dtype
    bf16 = jnp.bfloat16
    w_in_p = (jnp.pad(w_in[:, :, :N_MAIN].astype(bf16), ((0, 0), (0, 0), (0, N_MAIN_PAD - N_MAIN))),
              w_in[:, :, N_MAIN:].astype(bf16))
    w_branch = w_branch.astype(bf16)
    w_out = w_out.astype(bf16)
    hp = x_prompt
    hs = x_sample
    new_p = []
    new_s = []
    for l in range(DEPTH):
        rec_p = (jnp.zeros((bp, H_A, DK_A, DV_A), dt), jnp.zeros((bp, H_C, DH_C, DH_C), dt),
                 jnp.zeros((bp, H_C, DH_C), dt), jnp.zeros((bp, H_C), dt),
                 jnp.zeros((bp, CONV_W - 1, 2 * W_BR), dt))
        hp, st_p = _sublayer(hp, pos_p, l, w_in_p, w_a_gate, b_a_gate, g_a, lam_qk, g_b, b_c_if,
                             w_c_conv, g_c, w_branch, w_out, ln_g, ln_b, rec_p, None)
        rec_s = (state_a_s[l], state_c_c[l], state_c_n[l], state_c_m[l], state_c_conv[l])
        paged = (page_table, cache_b_k, cache_b_v, cache_d_k, cache_d_v, cache_d_ki)
        hs, st_s = _sublayer(hs, pos_s, l, w_in_p, w_a_gate, b_a_gate, g_a, lam_qk, g_b, b_c_if,
                             w_c_conv, g_c, w_branch, w_out, ln_g, ln_b, rec_s, paged)
        new_p.append(st_p)
        new_s.append(st_s)
    outs_p = [jnp.stack([st[i] for st in new_p], 0) for i in range(10)]
    outs_s = [jnp.stack([st[i] for st in new_s], 0) for i in range(10)]
    return (hp, hs, *outs_p, *outs_s)
```

```python
import functools
import math
import jax
import jax.numpy as jnp
from jax import lax
import numpy as np
from jax.experimental import pallas as pl
from jax.experimental.pallas import tpu as pltpu

D_MODEL = 2048
DEPTH = 2
PAGE_SIZE = 128
N_BRANCH = 4
W_BR = D_MODEL // 4
H_A = 4
DK_A = W_BR // (2 * H_A)
DV_A = W_BR // H_A
GATE_RANK = 16
GATE_TAU = 16.0
GLA_CHUNK = 16
H_B = 4
DH_B = W_BR // (2 * H_B)
H_C = 4
DH_C = W_BR // H_C
CONV_W = 4
MLSTM_CHUNK = 64
H_D = 4
DH_D = W_BR // H_D
H_I = 4
D_I = 64
TOPK_MAX = 256
Q_BLOCK = 128
ROPE_THETA = 10000.0
LN_EPS = 1e-5
NORM_EPS = 1e-6
DN_ALPHA = (2.0 * DEPTH) ** 0.25

IN_WIDTHS = (
    ('a_q', H_A * DK_A), ('a_k', H_A * DK_A), ('a_v', W_BR), ('a_g', GATE_RANK), ('a_z', W_BR),
    ('b_q', W_BR), ('b_k', W_BR), ('b_v', W_BR), ('b_z', W_BR),
    ('c_q', W_BR), ('c_k', W_BR), ('c_v', W_BR), ('c_i', H_C), ('c_f', H_C), ('c_o', W_BR), ('c_z', W_BR),
    ('d_q', W_BR), ('d_k', W_BR), ('d_v', W_BR), ('d_qi', H_I * D_I), ('d_ki', D_I), ('d_w', H_I), ('d_z', W_BR),
    ('gate', N_BRANCH * D_MODEL),
)
N_IN = sum(w for _, w in IN_WIDTHS)
N_MAIN = N_IN - N_BRANCH * D_MODEL
LANES = 128
N_MAIN_PAD = -(-N_MAIN // (4 * LANES)) * (4 * LANES)


def _mm_kernel(x_ref, w_ref, o_ref):
    o_ref[...] = jnp.dot(x_ref[...], w_ref[...], preferred_element_type=jnp.float32)


def _matmul(x, w, tm, tn):
    m, k = x.shape
    n = w.shape[1]
    return pl.pallas_call(
        _mm_kernel,
        grid=(n // tn, m // tm),
        in_specs=[pl.BlockSpec((tm, k), lambda j, i: (i, 0)),
                  pl.BlockSpec((k, tn), lambda j, i: (0, j))],
        out_specs=pl.BlockSpec((tm, tn), lambda j, i: (i, j)),
        out_shape=jax.ShapeDtypeStruct((m, n), jnp.float32),
        compiler_params=pltpu.CompilerParams(
            dimension_semantics=("arbitrary", "arbitrary"), vmem_limit_bytes=48 * 1024 * 1024),
        name="in_proj",
    )(x, w)


NEG_BIG = -1e30
INT_MIN = -2 ** 31
_NT = (((1,), (1,)), ((), ()))
VMEM_LIMIT = 48 * 1024 * 1024


def _split3(x):
    hi = x.astype(jnp.bfloat16)
    lo = (x - hi.astype(jnp.float32)).astype(jnp.bfloat16)
    return hi, lo


def _flash_consume(s, vh, carry):
    m, l, acc = carry
    m_new = jnp.maximum(m, jnp.max(s, axis=1, keepdims=True))
    alpha = jnp.exp(m - m_new)
    p = jnp.exp(s - m_new)
    l = alpha * l + jnp.sum(p, axis=1, keepdims=True)
    acc = alpha * acc + jnp.dot(p.astype(jnp.bfloat16), vh, preferred_element_type=jnp.float32)
    return m_new, l, acc


def _flash_consume_t(s_t, v_t, carry):
    m, l, acc = carry
    m_new = jnp.maximum(m, jnp.max(s_t, axis=0, keepdims=True))
    alpha = jnp.exp(m - m_new)
    p = jnp.exp(s_t - m_new)
    l = alpha * l + jnp.sum(p, axis=0, keepdims=True)
    acc = alpha * acc + jnp.dot(v_t, p.astype(jnp.bfloat16), preferred_element_type=jnp.float32)
    return m_new, l, acc


def _flash_init_t(tq, dv):
    return (jnp.full((1, tq), NEG_BIG, jnp.float32), jnp.zeros((1, tq), jnp.float32),
            jnp.zeros((dv, tq), jnp.float32))


def _chunked_t(v, kc):
    b, t, w = v.shape
    return v.astype(jnp.bfloat16).reshape(b, t // kc, kc, w).transpose(0, 1, 3, 2)


def _flash_init(tq, dv):
    return (jnp.full((tq, 1), NEG_BIG, jnp.float32), jnp.zeros((tq, 1), jnp.float32),
            jnp.zeros((tq, dv), jnp.float32))


def _dsa_prompt_kernel(q_ref, k_ref, v_ref, qi_ref, ki_ref, wi_ref, tri_ref, o_ref, key_ref, bias_ref,
                       *, tq, kc, n_sel, scale):
    f32 = jnp.float32
    i = pl.program_id(1)
    nc = (i * tq + tq + kc - 1) // kc
    qpos = i * tq + lax.broadcasted_iota(jnp.int32, (tq, 1), 0)
    lane = lax.broadcasted_iota(jnp.int32, (1, kc), 1)
    wi = wi_ref[0] * (H_I ** -0.5 * D_I ** -0.5)
    wis = [jnp.broadcast_to(wi[:, h:h + 1], (tq, kc)) for h in range(H_I)]
    d3 = 3 * D_I

    def score_body(c, carry):
        off = pl.multiple_of(c * kc, kc)
        kik = ki_ref[0, pl.ds(off, kc), :]
        acc = jnp.zeros((tq, kc), f32)
        for h in range(H_I):
            s = lax.dot_general(qi_ref[0, :, h * d3:(h + 1) * d3], kik, _NT, preferred_element_type=f32)
            acc = acc + wis[h] * jnp.maximum(s, 0.0)
        acc = jnp.where(off + lane <= qpos, acc, -jnp.inf)
        bits = pltpu.bitcast(acc, jnp.int32)
        bits = jnp.where(bits == INT_MIN, 0, bits)
        key_ref[c] = jnp.where(bits < 0, bits ^ 0x7FFFFFFF, bits)
        return carry
    lax.fori_loop(0, nc, score_body, 0)

    def count(pred, ref_val):
        def body(c, part):
            hit = jnp.where(pred(key_ref[c], ref_val), 1.0, 0.0)
            for j in range(kc // LANES):
                part = part + hit[:, j * LANES:(j + 1) * LANES]
            return part
        part = lax.fori_loop(0, nc, body, jnp.zeros((tq, LANES), f32))
        return jnp.sum(part, axis=1, keepdims=True)

    def bit_body(it, tau):
        cand = tau ^ jnp.left_shift(jnp.int32(1), 31 - it)
        return jnp.where(count(lambda a, b: a >= b, cand) >= n_sel, cand, tau)
    tau = lax.fori_loop(0, 32, bit_body, jnp.full((tq, 1), INT_MIN, jnp.int32))
    need = n_sel - count(lambda a, b: a > b, tau)

    def bias_body(c, seen):
        key = key_ref[c]
        tie = key == tau
        tie_f = jnp.where(tie, 1.0, 0.0)
        before = seen + jnp.dot(tie_f.astype(jnp.bfloat16), tri_ref[...], preferred_element_type=f32)
        sel = ((key > tau) | (tie & (before < need))) & (c * kc + lane <= qpos)
        bias_ref[c] = jnp.where(sel, 0.0, NEG_BIG)
        return seen + jnp.sum(tie_f, axis=1, keepdims=True)
    lax.fori_loop(0, nc, bias_body, jnp.zeros((tq, 1), f32))

    heads = [slice(h * DH_D, (h + 1) * DH_D) for h in range(H_D)]
    qs = [(q_ref[0, :, cols] * scale).astype(jnp.bfloat16) for cols in heads]

    def logits(c):
        rows = pl.ds(pl.multiple_of(c * kc, kc), kc)
        return tuple(lax.dot_general(qs[h], k_ref[0, rows, heads[h]], _NT, preferred_element_type=f32)
                     for h in range(H_D))

    def att_body(c, carry):
        s_cur, states = carry
        s_next = logits(jnp.minimum(c + 1, nc - 1))
        rows = pl.ds(pl.multiple_of(c * kc, kc), kc)
        bias = bias_ref[c]
        return s_next, tuple(_flash_consume(s_cur[h] + bias, v_ref[0, rows, heads[h]], states[h]) for h in range(H_D))
    _, fin = lax.fori_loop(0, nc, att_body, (logits(0), tuple(_flash_init(tq, DH_D) for _ in range(H_D))))
    for h in range(H_D):
        o_ref[0, :, heads[h]] = fin[h][2] / fin[h][1]


def _dsa_prompt_pallas(q, k, v, qi, ki, wi, n_sel, tq=128, kc=256):
    b, t, w = q.shape
    bf16 = jnp.bfloat16
    qh, ql = _split3(qi.reshape(b, t, H_I, D_I))
    kh, kl = _split3(ki)
    qi3 = jnp.concatenate([qh, qh, ql], -1).reshape(b, t, H_I * 3 * D_I)
    ki3 = jnp.concatenate([kh, kl, kh], -1)
    tri = jnp.asarray(np.triu(np.ones((kc, kc), np.float32), 1), bf16)
    kern = functools.partial(_dsa_prompt_kernel, tq=tq, kc=kc, n_sel=n_sel, scale=DH_D ** -0.5)
    blk = lambda bi, i: (bi, i, 0)
    full = lambda bi, i: (bi, 0, 0)
    return pl.pallas_call(
        kern,
        grid=(b, t // tq),
        in_specs=[pl.BlockSpec((1, tq, w), blk), pl.BlockSpec((1, t, w), full), pl.BlockSpec((1, t, w), full),
                  pl.BlockSpec((1, tq, H_I * 3 * D_I), blk), pl.BlockSpec((1, t, 3 * D_I), full),
                  pl.BlockSpec((1, tq, H_I), blk), pl.BlockSpec((kc, kc), lambda bi, i: (0, 0))],
        out_specs=pl.BlockSpec((1, tq, w), blk),
        out_shape=jax.ShapeDtypeStruct((b, t, w), jnp.float32),
        scratch_shapes=[pltpu.VMEM((t // kc, tq, kc), jnp.int32), pltpu.VMEM((t // kc, tq, kc), jnp.float32)],
        compiler_params=pltpu.CompilerParams(
            dimension_semantics=("arbitrary", "arbitrary"), vmem_limit_bytes=VMEM_LIMIT),
        name="dsa_prompt",
    )(q, k.astype(bf16), v.astype(bf16), qi3, ki3, wi, tri)


def _diff_prompt_kernel(lam_ref, q_ref, k_ref, vt_ref, o_ref, *, tq, kc, scale):
    i = pl.program_id(1)
    nc = (i * tq + tq + kc - 1) // kc
    qpos = i * tq + lax.broadcasted_iota(jnp.int32, (1, tq), 1)
    kpos = lax.broadcasted_iota(jnp.int32, (kc, 1), 0)
    dcol = lax.broadcasted_iota(jnp.int32, (1, 2 * DH_B), 1)
    lam = lam_ref[0, 0]
    for pair in range(H_B // 2):
        heads = [slice(h * 2 * DH_B, (h + 1) * 2 * DH_B) for h in (2 * pair, 2 * pair + 1)]
        qms = [jnp.where((dcol >= m * DH_B) & (dcol < (m + 1) * DH_B), q_ref[0, :, cols] * scale, 0.0).astype(jnp.bfloat16)
               for cols in heads for m in range(2)]

        def logits(c, heads=heads, qms=qms):
            rows = pl.ds(pl.multiple_of(c * kc, kc), kc)
            return tuple(lax.dot_general(k_ref[0, rows, cols], qms[2 * j + m], _NT, preferred_element_type=jnp.float32)
                         for j, cols in enumerate(heads) for m in range(2))

        def att_body(c, carry, heads=heads, logits=logits):
            s_cur, states = carry
            s_next = logits(jnp.minimum(c + 1, nc - 1))
            bias = jnp.where(c * kc + kpos <= qpos, 0.0, NEG_BIG)
            out = []
            for j, cols in enumerate(heads):
                vh = vt_ref[0, c, cols, :]
                for m in range(2):
                    out.append(_flash_consume_t(s_cur[2 * j + m] + bias, vh, states[2 * j + m]))
            return s_next, tuple(out)
        _, fin = lax.fori_loop(0, nc, att_body, (logits(0), tuple(_flash_init_t(tq, 2 * DH_B) for _ in range(4))))
        for j, cols in enumerate(heads):
            o_ref[0, cols, :] = fin[2 * j][2] / fin[2 * j][1] - lam * (fin[2 * j + 1][2] / fin[2 * j + 1][1])


def _diff_prompt_pallas(q, k, v, lam, tq=128, kc=256):
    b, t, w = q.shape
    bf16 = jnp.bfloat16
    kern = functools.partial(_diff_prompt_kernel, tq=tq, kc=kc, scale=DH_B ** -0.5)
    blk = lambda bi, i: (bi, i, 0)
    full = lambda bi, i: (bi, 0, 0)
    return pl.pallas_call(
        kern,
        grid=(b, t // tq),
        in_specs=[pl.BlockSpec(memory_space=pltpu.SMEM),
                  pl.BlockSpec((1, tq, w), blk), pl.BlockSpec((1, t, w), full),
                  pl.BlockSpec((1, t // kc, w, kc), lambda bi, i: (bi, 0, 0, 0))],
        out_specs=pl.BlockSpec((1, w, tq), lambda bi, i: (bi, 0, i)),
        out_shape=jax.ShapeDtypeStruct((b, w, t), jnp.float32),
        compiler_params=pltpu.CompilerParams(
            dimension_semantics=("arbitrary", "arbitrary"), vmem_limit_bytes=VMEM_LIMIT),
        name="diff_prompt",
    )(lam.reshape(1, 1).astype(jnp.float32), q, k.astype(bf16), _chunked_t(v, kc)).transpose(0, 2, 1)


PPS = 16
TPAD = 8
NEG_INF_KEY = INT_MIN + 0x7FFFFF


def _float_order_key(x):
    bits = pltpu.bitcast(x, jnp.int32)
    bits = jnp.where(bits == INT_MIN, 0, bits)
    return jnp.where(bits < 0, bits ^ 0x7FFFFFFF, bits)


def _page_spec(l, r, rows, n_pages):
    def index(b, j, pt):
        return (l, pt[b * n_pages + j * PPS + r], 0, 0)
    return pl.BlockSpec((1, 1, rows, PAGE_SIZE), index)


def _rows_view(cache):
    return cache.reshape(cache.shape[0], cache.shape[1], cache.shape[2] * cache.shape[3], cache.shape[4])


def _online_softmax_step(state, s_list, pv):
    m_old, l, acc = state
    m_new = m_old
    for s in s_list:
        m_new = jnp.maximum(m_new, jnp.max(s, axis=1, keepdims=True))
    alpha = jnp.exp(m_old - m_new)
    p_list = [jnp.exp(s - m_new) for s in s_list]
    l = alpha * l
    for p in p_list:
        l = l + jnp.sum(p, axis=1, keepdims=True)
    return m_new, l, alpha * acc + pv(p_list)


def _online_softmax_update(m_ref, l_ref, acc_ref, s_list, pv):
    m_ref[...], l_ref[...], acc_ref[...] = _online_softmax_step((m_ref[...], l_ref[...], acc_ref[...]), s_list, pv)


def _sample_score_kernel(pt_ref, qh_ref, ql_ref, w_ref, kn_ref, *rest):
    ki_refs, (o_ref, on_ref) = rest[:PPS], rest[PPS:]
    f32 = jnp.float32
    qh, ql, w = qh_ref[0], ql_ref[0], w_ref[0]

    def scores(kt):
        kh, kl = _split3(kt)
        s = (jnp.dot(qh, kh, preferred_element_type=f32) + jnp.dot(qh, kl, preferred_element_type=f32)
             + jnp.dot(ql, kh, preferred_element_type=f32))
        r = w * jnp.maximum(s, 0.0)
        out = r[0:TPAD]
        for h in range(1, H_I):
            out = out + r[h * TPAD:(h + 1) * TPAD]
        return out
    for r in range(PPS):
        o_ref[0, :, r * PAGE_SIZE:(r + 1) * PAGE_SIZE] = scores(ki_refs[r][0, 0])

    @pl.when(pl.program_id(1) == 0)
    def _():
        t = lax.broadcasted_iota(jnp.int32, (TPAD, PAGE_SIZE), 0)
        j = lax.broadcasted_iota(jnp.int32, (TPAD, PAGE_SIZE), 1)
        on_ref[0] = jnp.where(j <= t, scores(kn_ref[0]), -jnp.inf)


def _sample_scores_pallas(l, qi, ki, wi, cache_ki, page_table):
    b, t = qi.shape[:2]
    n_pages = page_table.shape[1]
    qpad = jnp.zeros((b, H_I, TPAD, D_I), jnp.float32).at[:, :, :t].set(qi.transpose(0, 2, 1, 3))
    qh, ql = _split3(qpad.reshape(b, H_I * TPAD, D_I))
    wcol = jnp.zeros((b, H_I, TPAD), jnp.float32).at[:, :, :t].set(wi.transpose(0, 2, 1))
    wcol = (wcol * (H_I ** -0.5 * D_I ** -0.5)).reshape(b, H_I * TPAD, 1)
    knew = jnp.zeros((b, D_I, PAGE_SIZE), jnp.float32).at[:, :, :t].set(ki.transpose(0, 2, 1))
    row = lambda bi, j, pt: (bi, 0, 0)
    grid_spec = pltpu.PrefetchScalarGridSpec(
        num_scalar_prefetch=1, grid=(b, n_pages // PPS),
        in_specs=[pl.BlockSpec((1, H_I * TPAD, D_I), row), pl.BlockSpec((1, H_I * TPAD, D_I), row),
                  pl.BlockSpec((1, H_I * TPAD, 1), row), pl.BlockSpec((1, D_I, PAGE_SIZE), row)]
                 + [_page_spec(l, r, D_I, n_pages) for r in range(PPS)],
        out_specs=[pl.BlockSpec((1, TPAD, PPS * PAGE_SIZE), lambda bi, j, pt: (bi, 0, j)),
                   pl.BlockSpec((1, TPAD, PAGE_SIZE), row)])
    return pl.pallas_call(
        _sample_score_kernel, grid_spec=grid_spec,
        out_shape=[jax.ShapeDtypeStruct((b, TPAD, n_pages * PAGE_SIZE), jnp.float32),
                   jax.ShapeDtypeStruct((b, TPAD, PAGE_SIZE), jnp.float32)],
        compiler_params=pltpu.CompilerParams(dimension_semantics=("arbitrary", "arbitrary")),
        name="dsa_sample_scores",
    )(page_table.reshape(-1), qh, ql, wcol, knew, *([cache_ki.transpose(0, 1, 3, 2)] * PPS))


def _topk_bias_kernel(sc_ref, tri_ref, bias_ref, key_ref, *, n_sel):
    f32 = jnp.float32
    nc, rows, kc = sc_ref.shape

    def key_body(c, carry):
        key_ref[c] = _float_order_key(sc_ref[c])
        return carry
    lax.fori_loop(0, nc, key_body, 0)

    def count(pred, ref_val):
        def body(c, part):
            hit = jnp.where(pred(key_ref[c], ref_val), 1.0, 0.0)
            for j in range(kc // LANES):
                part = part + hit[:, j * LANES:(j + 1) * LANES]
            return part
        part = lax.fori_loop(0, nc, body, jnp.zeros((rows, LANES), f32))
        return jnp.sum(part, axis=1, keepdims=True)

    def bit_body(it, tau):
        cand = tau ^ jnp.left_shift(jnp.int32(1), 31 - it)
        return jnp.where(count(lambda a, b: a >= b, cand) >= n_sel, cand, tau)
    tau = lax.fori_loop(0, 32, bit_body, jnp.full((rows, 1), INT_MIN, jnp.int32))
    need = n_sel - count(lambda a, b: a > b, tau)

    def bias_body(c, seen):
        key = key_ref[c]
        tie = key == tau
        tie_f = jnp.where(tie, 1.0, 0.0)
        before = seen + jnp.dot(tie_f.astype(jnp.bfloat16), tri_ref[...], preferred_element_type=f32)
        sel = ((key > tau) | (tie & (before < need))) & (key > NEG_INF_KEY)
        bias_ref[c] = jnp.where(sel, 0.0, NEG_BIG)
        return seen + jnp.sum(tie_f, axis=1, keepdims=True)
    lax.fori_loop(0, nc, bias_body, jnp.zeros((rows, 1), f32))


def _topk_bias_pallas(scores, n_sel, kc=256):
    rows, n = scores.shape
    nc = n // kc
    tri = jnp.asarray(np.triu(np.ones((kc, kc), np.float32), 1), jnp.bfloat16)
    bias = pl.pallas_call(
        functools.partial(_topk_bias_kernel, n_sel=n_sel),
        out_shape=jax.ShapeDtypeStruct((nc, rows, kc), jnp.float32),
        scratch_shapes=[pltpu.VMEM((nc, rows, kc), jnp.int32)],
        compiler_params=pltpu.CompilerParams(vmem_limit_bytes=VMEM_LIMIT),
        name="topk_bias",
    )(scores.reshape(rows, nc, kc).transpose(1, 0, 2), tri)
    return bias.transpose(1, 0, 2).reshape(rows, n)


def _paged_attn_kernel(pt_ref, q_ref, kn_ref, vn_ref, bn_ref, *rest, scale, has_bias):
    f32 = jnp.float32
    bf16 = jnp.bfloat16
    if has_bias:
        bp_ref, rest = rest[0], rest[1:]
    k_refs, v_refs = rest[:PPS], rest[PPS:2 * PPS]
    o_ref, m_ref, l_ref, acc_ref = rest[2 * PPS:]
    j = pl.program_id(1)

    @pl.when(j == 0)
    def _():
        m_ref[...] = jnp.full(m_ref.shape, NEG_BIG, f32)
        l_ref[...] = jnp.zeros(l_ref.shape, f32)
        acc_ref[...] = jnp.zeros(acc_ref.shape, f32)

    q = q_ref[0]

    def update(s_list, v_list):
        m_old = m_ref[...]
        m_new = m_old
        for s in s_list:
            m_new = jnp.maximum(m_new, jnp.max(s, axis=1, keepdims=True))
        alpha = jnp.exp(m_old - m_new)
        l = alpha * l_ref[...]
        acc = alpha * acc_ref[...]
        for s, v in zip(s_list, v_list):
            p = jnp.exp(s - m_new)
            l = l + jnp.sum(p, axis=1, keepdims=True)
            acc = acc + jnp.dot(p.astype(bf16), v, preferred_element_type=f32)
        m_ref[...] = m_new
        l_ref[...] = l
        acc_ref[...] = acc

    s_list, v_list = [], []
    for r in range(PPS):
        s = lax.dot_general(q, k_refs[r][0, 0].astype(bf16), _NT, preferred_element_type=f32) * scale
        if has_bias:
            s = s + bp_ref[0, :, r * PAGE_SIZE:(r + 1) * PAGE_SIZE]
        s_list.append(s)
        v_list.append(v_refs[r][0, 0].astype(bf16))
    update(s_list, v_list)

    @pl.when(j == pl.num_programs(1) - 1)
    def _():
        s = lax.dot_general(q, kn_ref[0], _NT, preferred_element_type=f32) * scale + bn_ref[0]
        update([s], [vn_ref[0]])
        o_ref[0] = acc_ref[...] / l_ref[...]


def _paged_attn_pallas(l, qbd, k_new, v_new, bias_new, bias_past, cache_k, cache_v, page_table, scale):
    b, rws, w = qbd.shape
    t = k_new.shape[1]
    n_pages = page_table.shape[1]
    bf16 = jnp.bfloat16
    kn = jnp.zeros((b, TPAD, w), bf16).at[:, :t].set(k_new.astype(bf16))
    vn = jnp.zeros((b, TPAD, w), bf16).at[:, :t].set(v_new.astype(bf16))
    bn = jnp.full((b, rws, TPAD), NEG_BIG, jnp.float32).at[:, :, :t].set(bias_new)
    row = lambda bi, j, pt: (bi, 0, 0)
    in_specs = [pl.BlockSpec((1, rws, w), row), pl.BlockSpec((1, TPAD, w), row), pl.BlockSpec((1, TPAD, w), row),
                pl.BlockSpec((1, rws, TPAD), row)]
    args = [qbd.astype(bf16), kn, vn, bn]
    if bias_past is not None:
        in_specs.append(pl.BlockSpec((1, rws, PPS * PAGE_SIZE), lambda bi, j, pt: (bi, 0, j)))
        args.append(bias_past)
    in_specs += [_page_spec(l, r, w, n_pages) for r in range(PPS)] * 2
    args += [cache_k] * PPS + [cache_v] * PPS
    grid_spec = pltpu.PrefetchScalarGridSpec(
        num_scalar_prefetch=1, grid=(b, n_pages // PPS), in_specs=in_specs,
        out_specs=pl.BlockSpec((1, rws, w), row),
        scratch_shapes=[pltpu.VMEM((rws, 1), jnp.float32), pltpu.VMEM((rws, 1), jnp.float32),
                        pltpu.VMEM((rws, w), jnp.float32)])
    return pl.pallas_call(
        functools.partial(_paged_attn_kernel, scale=scale, has_bias=bias_past is not None),
        grid_spec=grid_spec,
        out_shape=jax.ShapeDtypeStruct((b, rws, w), jnp.float32),
        compiler_params=pltpu.CompilerParams(
            dimension_semantics=("arbitrary", "arbitrary"), vmem_limit_bytes=VMEM_LIMIT),
        name="paged_attn",
    )(page_table.reshape(-1), *args)


def _head_rows(x, n_blocks):
    b, t, w = x.shape
    blk = jnp.arange(w) // (w // n_blocks)
    keep = blk[None, :] == jnp.arange(n_blocks)[:, None]
    return jnp.where(keep[None, :, None, :], x[:, None, :, :], 0.0).reshape(b, n_blocks * t, w)


def _dsa_sample_pallas(l, q, k, v, qi, ki, wi_raw, cache_ki, cache_k, cache_v, page_table):
    b, t, h, dh = q.shape
    p_len = page_table.shape[1] * PAGE_SIZE
    n_sel = min(TOPK_MAX, (p_len + t) // 4)
    s_past, s_new = _sample_scores_pallas(l, qi, ki, wi_raw, cache_ki, page_table)
    pad = jnp.full((b, t, PAGE_SIZE), -jnp.inf, jnp.float32)
    scores = jnp.concatenate([s_past[:, :t], s_new[:, :t], pad], -1).reshape(b * t, p_len + 2 * PAGE_SIZE)
    bias = _topk_bias_pallas(scores, n_sel).reshape(b, t, -1)
    tile = lambda z: jnp.tile(z, (1, h, 1))
    w = h * dh
    res = _paged_attn_pallas(l, _head_rows(q.reshape(b, t, w), h), k.reshape(b, t, w), v.reshape(b, t, w),
                             tile(bias[:, :, p_len:p_len + t]), tile(bias[:, :, :p_len]),
                             cache_k.reshape(cache_k.shape[:3] + (w,)), cache_v.reshape(cache_v.shape[:3] + (w,)),
                             page_table, dh ** -0.5)
    res = res.reshape(b, h, t, h, dh)
    return jnp.stack([res[:, i, :, i] for i in range(h)], 2)


def _diff_sample_pallas(l, q, k, v, cache_k, cache_v, page_table, lam):
    b, t, h = q.shape[:3]
    w = h * 2 * DH_B
    causal = jnp.where(jnp.arange(t)[None, :] <= jnp.arange(t)[:, None], 0.0, NEG_BIG)
    bias_new = jnp.broadcast_to(jnp.tile(causal, (2 * h, 1))[None], (b, 2 * h * t, t))
    res = _paged_attn_pallas(l, _head_rows(q.reshape(b, t, w), 2 * h), k.reshape(b, t, w), v.reshape(b, t, w),
                             bias_new, None,
                             cache_k.reshape(cache_k.shape[:3] + (w,)), cache_v.reshape(cache_v.shape[:3] + (w,)),
                             page_table, DH_B ** -0.5)
    res = res.reshape(b, h, 2, t, h, 2 * DH_B)
    return jnp.stack([res[:, i, 0, :, i] - lam * res[:, i, 1, :, i] for i in range(h)], 2)


def _dsa_sample_attn_kernel(pt_ref, q_ref, kn_ref, vn_ref, bn_ref, bp_ref, *rest, scale):
    f32, bf16 = jnp.float32, jnp.bfloat16
    k_refs, v_refs = rest[:PPS], rest[PPS:2 * PPS]
    o_ref, m_ref, l_ref, acc_ref = rest[2 * PPS:]
    j = pl.program_id(1)
    wide = PAGE_SIZE * H_D

    @pl.when(j == 0)
    def _():
        m_ref[...] = jnp.full(m_ref.shape, NEG_BIG, f32)
        l_ref[...] = jnp.zeros(l_ref.shape, f32)
        acc_ref[...] = jnp.zeros(acc_ref.shape, f32)

    q = q_ref[0]
    s_list = [lax.dot_general(q, k_refs[r][0, 0].astype(bf16), _NT, preferred_element_type=f32) * scale
              + bp_ref[0, :, r * wide:(r + 1) * wide] for r in range(PPS)]

    def pv(p_list):
        out = jnp.dot(p_list[0].astype(bf16), v_refs[0][0, 0].astype(bf16), preferred_element_type=f32)
        for r in range(1, PPS):
            out = out + jnp.dot(p_list[r].astype(bf16), v_refs[r][0, 0].astype(bf16), preferred_element_type=f32)
        return out
    _online_softmax_update(m_ref, l_ref, acc_ref, s_list, pv)

    @pl.when(j == pl.num_programs(1) - 1)
    def _():
        s = lax.dot_general(q, kn_ref[0], _NT, preferred_element_type=f32) * scale + bn_ref[0]
        _online_softmax_update(m_ref, l_ref, acc_ref, [s],
                               lambda p: jnp.dot(p[0].astype(bf16), vn_ref[0], preferred_element_type=f32))
        o_ref[0] = acc_ref[...] / l_ref[...]


def _pad_tokens(x, fill=0.0):
    b, t = x.shape[:2]
    return jnp.full((b, TPAD) + x.shape[2:], fill, x.dtype).at[:, :t].set(x)


def _dsa_sample_pallas(l, q, k, v, qi, ki, wi_raw, cache_ki, cache_k, cache_v, page_table):
    b, t, h, dh = q.shape
    n_pages = page_table.shape[1]
    p_len = n_pages * PAGE_SIZE
    n_sel = min(TOPK_MAX, (p_len + t) // 4)
    bf16 = jnp.bfloat16
    s_past, s_new = _sample_scores_pallas(l, qi, ki, wi_raw, cache_ki, page_table)
    pad = jnp.full((b, t, PAGE_SIZE), -jnp.inf, jnp.float32)
    scores = jnp.concatenate([s_past[:, :t], s_new[:, :t], pad], -1).reshape(b * t, p_len + 2 * PAGE_SIZE)
    bias = _topk_bias_pallas(scores, n_sel).reshape(b, t, -1)
    rows = h * t
    same_head = jnp.arange(h)[:, None] == jnp.arange(h)[None, :]

    def per_head_bias(bz):
        full = jnp.where(same_head[None, :, None, None, :], bz[:, None, :, :, None], NEG_BIG)
        return full.reshape(b, rows, bz.shape[-1] * h)
    new_rows = TPAD * h
    bias_new = per_head_bias(jnp.full((b, t, TPAD), NEG_BIG, jnp.float32).at[:, :, :t].set(bias[:, :, p_len:p_len + t]))
    key_rows = lambda z: jnp.zeros((b, new_rows, dh), bf16).at[:, :t * h].set(z.reshape(b, t * h, dh).astype(bf16))
    row3 = lambda bi, j, pt: (bi, 0, 0)
    grid_spec = pltpu.PrefetchScalarGridSpec(
        num_scalar_prefetch=1, grid=(b, n_pages // PPS),
        in_specs=[pl.BlockSpec((1, rows, dh), row3), pl.BlockSpec((1, new_rows, dh), row3),
                  pl.BlockSpec((1, new_rows, dh), row3), pl.BlockSpec((1, rows, new_rows), row3),
                  pl.BlockSpec((1, rows, PPS * PAGE_SIZE * h), lambda bi, j, pt: (bi, 0, j))]
                 + [_page_spec(l, r, PAGE_SIZE * h, n_pages) for r in range(PPS)] * 2,
        out_specs=pl.BlockSpec((1, rows, dh), row3),
        scratch_shapes=[pltpu.VMEM((rows, 1), jnp.float32), pltpu.VMEM((rows, 1), jnp.float32),
                        pltpu.VMEM((rows, dh), jnp.float32)])
    res = pl.pallas_call(
        functools.partial(_dsa_sample_attn_kernel, scale=dh ** -0.5), grid_spec=grid_spec,
        out_shape=jax.ShapeDtypeStruct((b, rows, dh), jnp.float32),
        compiler_params=pltpu.CompilerParams(
            dimension_semantics=("arbitrary", "arbitrary"), vmem_limit_bytes=VMEM_LIMIT),
        name="dsa_sample_attn",
    )(page_table.reshape(-1), q.transpose(0, 2, 1, 3).reshape(b, rows, dh).astype(bf16), key_rows(k), key_rows(v),
      bias_new, per_head_bias(bias[:, :, :p_len]), *([_rows_view(cache_k)] * PPS), *([_rows_view(cache_v)] * PPS))
    return res.reshape(b, h, t, dh).transpose(0, 2, 1, 3)


def _diff_sample_attn_kernel(pt_ref, q_ref, knt_ref, vn_ref, bn_ref, *rest, scale):
    f32, bf16 = jnp.float32, jnp.bfloat16
    kt_refs, v_refs = rest[:PPS], rest[PPS:2 * PPS]
    o_ref, m_ref, l_ref, acc_ref = rest[2 * PPS:]
    j = pl.program_id(1)
    rph = 2 * TPAD // 2

    @pl.when(j == 0)
    def _():
        m_ref[...] = jnp.full(m_ref.shape, NEG_BIG, f32)
        l_ref[...] = jnp.zeros(l_ref.shape, f32)
        acc_ref[...] = jnp.zeros(acc_ref.shape, f32)

    q = q_ref[0]
    s_list = [jnp.dot(q, kt_refs[r][0, 0].astype(bf16), preferred_element_type=f32) * scale for r in range(PPS)]

    def pv(p_list):
        outs = []
        for h in range(H_B):
            head = pl.ds(h, PAGE_SIZE, stride=H_B)
            rows = slice(h * rph, (h + 1) * rph)
            out = jnp.dot(p_list[0][rows].astype(bf16), v_refs[0][0, 0, head, :].astype(bf16), preferred_element_type=f32)
            for r in range(1, PPS):
                out = out + jnp.dot(p_list[r][rows].astype(bf16), v_refs[r][0, 0, head, :].astype(bf16),
                                    preferred_element_type=f32)
            outs.append(out)
        return jnp.concatenate(outs, axis=0)
    _online_softmax_update(m_ref, l_ref, acc_ref, s_list, pv)

    @pl.when(j == pl.num_programs(1) - 1)
    def _():
        s = jnp.dot(q, knt_ref[0], preferred_element_type=f32) * scale + bn_ref[0]

        def pv_new(p):
            return jnp.concatenate([jnp.dot(p[0][h * rph:(h + 1) * rph].astype(bf16), vn_ref[0, h],
                                            preferred_element_type=f32) for h in range(H_B)], axis=0)
        _online_softmax_update(m_ref, l_ref, acc_ref, [s], pv_new)
        o_ref[0] = acc_ref[...] / l_ref[...]


def _diff_sample_pallas(l, q, k, v, cache_k, cache_v, page_table, lam):
    b, t, h = q.shape[:3]
    n_pages = page_table.shape[1]
    w = h * 2 * DH_B
    dv = 2 * DH_B
    tp = TPAD // 2
    rows = 2 * h * tp
    bf16 = jnp.bfloat16
    blk = jnp.arange(w) // DH_B
    qpad = jnp.zeros((b, tp, w), jnp.float32).at[:, :t].set(q.reshape(b, t, w))
    qbd = jnp.where((blk[None, :] == jnp.arange(2 * h)[:, None])[None, :, None, :], qpad[:, None], 0.0)
    qbd = qbd.reshape(b, rows, w).astype(bf16)
    knt = jnp.zeros((b, w, TPAD), jnp.float32).at[:, :, :t].set(k.reshape(b, t, w).transpose(0, 2, 1)).astype(bf16)
    vn = _pad_tokens(v).transpose(0, 2, 1, 3).astype(bf16)
    causal = jnp.where(jnp.arange(TPAD)[None, :] <= jnp.arange(tp)[:, None], 0.0, NEG_BIG)
    causal = jnp.where(jnp.arange(TPAD)[None, :] < t, causal, NEG_BIG)
    bias_new = jnp.broadcast_to(jnp.tile(causal, (2 * h, 1))[None], (b, rows, TPAD))
    kt_view = cache_k.transpose(0, 1, 3, 4, 5, 2).reshape(cache_k.shape[0], cache_k.shape[1], w, PAGE_SIZE)
    row3 = lambda bi, j, pt: (bi, 0, 0)
    grid_spec = pltpu.PrefetchScalarGridSpec(
        num_scalar_prefetch=1, grid=(b, n_pages // PPS),
        in_specs=[pl.BlockSpec((1, rows, w), row3), pl.BlockSpec((1, w, TPAD), row3),
                  pl.BlockSpec((1, h, TPAD, dv), lambda bi, j, pt: (bi, 0, 0, 0)), pl.BlockSpec((1, rows, TPAD), row3)]
                 + [_page_spec(l, r, w, n_pages) for r in range(PPS)]
                 + [_page_spec(l, r, PAGE_SIZE * h, n_pages) for r in range(PPS)],
        out_specs=pl.BlockSpec((1, rows, dv), row3),
        scratch_shapes=[pltpu.VMEM((rows, 1), jnp.float32), pltpu.VMEM((rows, 1), jnp.float32),
                        pltpu.VMEM((rows, dv), jnp.float32)])
    res = pl.pallas_call(
        functools.partial(_diff_sample_attn_kernel, scale=DH_B ** -0.5), grid_spec=grid_spec,
        out_shape=jax.ShapeDtypeStruct((b, rows, dv), jnp.float32),
        compiler_params=pltpu.CompilerParams(
            dimension_semantics=("arbitrary", "arbitrary"), vmem_limit_bytes=VMEM_LIMIT),
        name="diff_sample_attn",
    )(page_table.reshape(-1), qbd, knt, vn, bias_new, *([kt_view] * PPS), *([_rows_view(cache_v)] * PPS))
    res = res.reshape(b, h, 2, tp, dv)[:, :, :, :t]
    return (res[:, :, 0] - lam * res[:, :, 1]).transpose(0, 2, 1, 3)


_TN = (((0,), (0,)), ((), ()))


def _silu(x):
    return x / (1.0 + jnp.exp(-x))


def _log_sigmoid(x):
    return jnp.minimum(x, 0.0) - jnp.log(1.0 + jnp.exp(-jnp.abs(x)))


def _head_rms_gate(o, g, z, d):
    outs = []
    for h in range(o.shape[1] // d):
        cols = slice(h * d, (h + 1) * d)
        oh = o[:, cols]
        outs.append(oh * lax.rsqrt(jnp.mean(oh * oh, axis=1, keepdims=True) + NORM_EPS))
    return jnp.concatenate(outs, axis=1) * g * _silu(z)


def _gla_kernel(q_ref, k_ref, v_ref, g_ref, wg_ref, bg_ref, z_ref, gn_ref, tril_ref, s0_ref,
                y_ref, sout_ref, st_ref, la_ref, o_ref, *, tb, t_valid):
    f32, bf16 = jnp.float32, jnp.bfloat16
    c = GLA_CHUNK

    @pl.when(pl.program_id(1) == 0)
    def _():
        st_ref[...] = s0_ref[0]

    gl = jnp.dot(g_ref[0].astype(bf16), wg_ref[...].astype(bf16), preferred_element_type=f32) + bg_ref[...]
    la = _log_sigmoid(gl) / GATE_TAU
    if t_valid < tb:
        la = jnp.where(lax.broadcasted_iota(jnp.int32, (tb, 1), 0) < t_valid, la, 0.0)
    la_ref[...] = la
    trow = lax.broadcasted_iota(jnp.int32, (c, 1), 0)
    lane_id = lax.broadcasted_iota(jnp.int32, (1, LANES), 1)

    def chunk(ci, carry):
        rows = pl.ds(pl.multiple_of(ci * c, c), c)
        cb = jnp.dot(tril_ref[...], la_ref[rows, :], precision=lax.Precision.HIGHEST, preferred_element_type=f32)
        c_last = cb[c - 1:c, :]
        qc = q_ref[0, rows, :] * (DK_A ** -0.5)
        kc = k_ref[0, rows, :]
        vc = v_ref[0, rows, :]
        qe = (qc * jnp.exp(cb)).astype(bf16)
        kdec = (kc * jnp.exp(c_last - cb)).astype(bf16)
        dec = jnp.exp(c_last)
        s_ts = [st_ref[h] for h in range(H_A)]
        intra = [[] for _ in range(H_A)]
        for pair in range(H_A // 2):
            slab = slice(pair * LANES, (pair + 1) * LANES)
            q_p, k_p, cb_p = qc[:, slab], kc[:, slab], cb[:, slab]
            for s in range(c):
                x = q_p * k_p[s:s + 1, :] * jnp.exp(cb_p - cb_p[s:s + 1, :])
                for half in range(2):
                    h = 2 * pair + half
                    col = jnp.sum(jnp.where((lane_id >= half * DK_A) & (lane_id < (half + 1) * DK_A), x, 0.0),
                                  axis=1, keepdims=True)
                    intra[h].append(jnp.where(trow >= s, col, 0.0) * vc[s:s + 1, h * DV_A:(h + 1) * DV_A])
        outs = []
        for h in range(H_A):
            kcols = slice(h * DK_A, (h + 1) * DK_A)
            parts = intra[h]
            while len(parts) > 1:
                parts = [parts[i] + parts[i + 1] for i in range(0, len(parts), 2)]
            outs.append(parts[0] + lax.dot_general(qe[:, kcols], s_ts[h].astype(bf16), _NT, preferred_element_type=f32))
            s_ts[h] = s_ts[h] * dec[:, kcols] + lax.dot_general(
                vc[:, h * DV_A:(h + 1) * DV_A].astype(bf16), kdec[:, kcols], _TN, preferred_element_type=f32)
        o_ref[rows, :] = jnp.concatenate(outs, axis=1)
        for h in range(H_A):
            st_ref[h] = s_ts[h]
        return carry
    lax.fori_loop(0, tb // c, chunk, 0)
    y_ref[0] = _head_rms_gate(o_ref[...], gn_ref[...], z_ref[0], DV_A)

    @pl.when(pl.program_id(1) == pl.num_programs(1) - 1)
    def _():
        sout_ref[0] = st_ref[...]


def _gla_pallas(q, k, v, g, z, w_gate, b_gate, g_norm, s0, tb):
    b, t, _ = q.shape
    tp = -(-t // tb) * tb
    pad = lambda x: x if tp == t else jnp.zeros((b, tp, x.shape[2]), x.dtype).at[:, :t].set(x)
    tril = jnp.asarray(np.tril(np.ones((GLA_CHUNK, GLA_CHUNK), np.float32)))
    blk = lambda w: pl.BlockSpec((1, tb, w), lambda bi, j: (bi, j, 0))
    const = lambda shape: pl.BlockSpec(shape, lambda bi, j: (0,) * len(shape))
    st_spec = pl.BlockSpec((1, H_A, DV_A, DK_A), lambda bi, j: (bi, 0, 0, 0))
    y, s_t = pl.pallas_call(
        functools.partial(_gla_kernel, tb=tb, t_valid=min(t, tb)),
        grid=(b, tp // tb),
        in_specs=[blk(H_A * DK_A), blk(H_A * DK_A), blk(W_BR), blk(GATE_RANK),
                  const((GATE_RANK, H_A * DK_A)), const((1, H_A * DK_A)), blk(W_BR), const((1, W_BR)),
                  const((GLA_CHUNK, GLA_CHUNK)), st_spec],
        out_specs=[blk(W_BR), st_spec],
        out_shape=[jax.ShapeDtypeStruct((b, tp, W_BR), jnp.float32),
                   jax.ShapeDtypeStruct((b, H_A, DV_A, DK_A), jnp.float32)],
        scratch_shapes=[pltpu.VMEM((H_A, DV_A, DK_A), jnp.float32), pltpu.VMEM((tb, H_A * DK_A), jnp.float32),
                        pltpu.VMEM((tb, W_BR), jnp.float32)],
        compiler_params=pltpu.CompilerParams(
            dimension_semantics=("arbitrary", "arbitrary"), vmem_limit_bytes=VMEM_LIMIT),
        name="gla",
    )(pad(q), pad(k), pad(v), pad(g), w_gate, b_gate.reshape(1, -1), pad(z), g_norm.reshape(1, -1), tril,
      s0.swapaxes(2, 3))
    return y[:, :t], s_t.swapaxes(2, 3)


CONV_PAD = 8


def _mlstm_kernel(cq_ref, ck_ref, cv_ref, co_ref, cz_ref, icol_ref, fcol_ref, irow_ref, frow_ref,
                  wq_ref, wk_ref, gn_ref, tril_ref, conv0_ref, c0_ref, n0_ref, m0_ref,
                  y_ref, conv_ref, c_ref, n_ref, m_ref, xq_ref, xk_ref, cm_ref, nv_ref, mv_ref, *, ch, t_valid):
    f32, bf16 = jnp.float32, jnp.bfloat16
    hi = lax.Precision.HIGHEST
    first = CONV_PAD - (CONV_W - 1)

    @pl.when(pl.program_id(1) == 0)
    def _():
        xq_ref[0:CONV_PAD, :] = conv0_ref[0, :, 0:W_BR]
        xk_ref[0:CONV_PAD, :] = conv0_ref[0, :, W_BR:2 * W_BR]
        cm_ref[...] = c0_ref[0]
        nv_ref[...] = n0_ref[0]
        mv_ref[...] = m0_ref[0]

    def conv(x_ref, new_ref, w_ref, cols):
        x_ref[CONV_PAD:CONV_PAD + ch, :] = new_ref[0]
        out = x_ref[first:first + ch, :] * w_ref[0:1, :]
        for j in range(1, CONV_W):
            out = out + x_ref[first + j:first + j + ch, :] * w_ref[j:j + 1, :]
        conv_ref[0, :, cols] = x_ref[first + t_valid:first + t_valid + CONV_W - 1, :]
        tail = x_ref[ch:ch + CONV_PAD, :]
        x_ref[0:CONV_PAD, :] = tail
        return _silu(out)
    qc = conv(xq_ref, cq_ref, wq_ref, slice(0, W_BR))
    kc = conv(xk_ref, ck_ref, wk_ref, slice(W_BR, 2 * W_BR)) * (DH_C ** -0.5)

    tril = tril_ref[...]
    fcum_c = jnp.dot(tril, _log_sigmoid(fcol_ref[0]), precision=hi, preferred_element_type=f32)
    fcum_r = lax.dot_general(_log_sigmoid(frow_ref[0]), tril, _NT, precision=hi, preferred_element_type=f32)
    i_c, i_r = icol_ref[0], irow_ref[0]
    causal = lax.broadcasted_iota(jnp.int32, (ch, ch), 1) <= lax.broadcasted_iota(jnp.int32, (ch, ch), 0)
    outs = []
    for h in range(H_C):
        cols = slice(h * DH_C, (h + 1) * DH_C)
        q_h, k_h, v_h = qc[:, cols], kc[:, cols], cv_ref[0, :, cols]
        cm, nv, m_prev = cm_ref[h], nv_ref[h:h + 1, :], mv_ref[h:h + 1, 0:1]
        fc, fr, ic, ir = fcum_c[:, h:h + 1], fcum_r[h:h + 1, :], i_c[:, h:h + 1], i_r[h:h + 1, :]
        dmat = jnp.where(causal, fc - fr + ir, NEG_BIG)
        inter = fc + m_prev
        m_t = jnp.maximum(inter, jnp.max(dmat, axis=1, keepdims=True))
        w_inter = jnp.exp(inter - m_t)
        qk = lax.dot_general(q_h.astype(bf16), k_h.astype(bf16), _NT, preferred_element_type=f32) * jnp.exp(dmat - m_t)
        num = (w_inter * jnp.dot(q_h.astype(bf16), cm.astype(bf16), preferred_element_type=f32)
               + jnp.dot(qk.astype(bf16), v_h.astype(bf16), preferred_element_type=f32))
        den = w_inter * jnp.sum(q_h * nv, axis=1, keepdims=True) + jnp.sum(qk, axis=1, keepdims=True)
        hv = num / jnp.maximum(jnp.abs(den), jnp.exp(-m_t))
        outs.append(hv / (1.0 + jnp.exp(-co_ref[0, :, cols])))
        f_last = fc[ch - 1:ch, :]
        m_new = jnp.maximum(f_last + m_prev, jnp.max(f_last - fr + ir, axis=1, keepdims=True))
        a = jnp.exp(f_last + m_prev - m_new)
        kw = k_h * jnp.exp(f_last - fc + ic - m_new)
        cm_ref[h] = a * cm + lax.dot_general(kw.astype(bf16), v_h.astype(bf16), _TN, preferred_element_type=f32)
        nv_ref[h:h + 1, :] = a * nv + jnp.sum(kw, axis=0, keepdims=True)
        mv_ref[h:h + 1, :] = jnp.broadcast_to(m_new, (1, LANES))
    y_ref[0] = _head_rms_gate(jnp.concatenate(outs, axis=1), gn_ref[...], cz_ref[0], DH_C)

    @pl.when(pl.program_id(1) == pl.num_programs(1) - 1)
    def _():
        c_ref[0] = cm_ref[...]
        n_ref[0] = nv_ref[...]
        m_ref[0] = mv_ref[...]


def _mlstm_pallas(cq, ck, cv, co, cz, i_pre, f_pre, w_conv, g_norm, conv0, c0, n0, m0, ch):
    b, t, _ = cq.shape
    tp = -(-t // ch) * ch
    f32 = jnp.float32
    pad = lambda x, fill=0.0: x if tp == t else jnp.full((b, tp, x.shape[2]), fill, x.dtype).at[:, :t].set(x)
    i_pre, f_pre = pad(i_pre, NEG_BIG), pad(f_pre, -NEG_BIG)
    rows8 = lambda x: jnp.zeros((b, 8, x.shape[2]), f32).at[:, :x.shape[1]].set(x)
    tril = jnp.asarray(np.tril(np.ones((ch, ch), np.float32)))
    conv_hist = jnp.zeros((b, CONV_PAD, 2 * W_BR), f32).at[:, CONV_PAD - (CONV_W - 1):].set(conv0)
    assert tp == t or tp == ch
    blk = lambda w: pl.BlockSpec((1, ch, w), lambda bi, j: (bi, j, 0))
    rowblk = pl.BlockSpec((1, 8, ch), lambda bi, j: (bi, 0, j))
    const = lambda shape: pl.BlockSpec(shape, lambda bi, j: (0,) * len(shape))
    per_b = lambda shape: pl.BlockSpec((1,) + shape, lambda bi, j: (bi,) + (0,) * len(shape))
    y, conv_s, c_f, n_f, m_f = pl.pallas_call(
        functools.partial(_mlstm_kernel, ch=ch, t_valid=t - (tp - ch)),
        grid=(b, tp // ch),
        in_specs=[blk(W_BR)] * 5 + [blk(H_C), blk(H_C), rowblk, rowblk,
                  const((CONV_W, W_BR)), const((CONV_W, W_BR)), const((1, W_BR)), const((ch, ch)),
                  per_b((CONV_PAD, 2 * W_BR)), per_b((H_C, DH_C, DH_C)), per_b((8, DH_C)), per_b((8, LANES))],
        out_specs=[blk(W_BR), per_b((CONV_W - 1, 2 * W_BR)), per_b((H_C, DH_C, DH_C)), per_b((8, DH_C)),
                   per_b((8, LANES))],
        out_shape=[jax.ShapeDtypeStruct((b, tp, W_BR), f32), jax.ShapeDtypeStruct((b, CONV_W - 1, 2 * W_BR), f32),
                   jax.ShapeDtypeStruct((b, H_C, DH_C, DH_C), f32), jax.ShapeDtypeStruct((b, 8, DH_C), f32),
                   jax.ShapeDtypeStruct((b, 8, LANES), f32)],
        scratch_shapes=[pltpu.VMEM((CONV_PAD + ch, W_BR), f32), pltpu.VMEM((CONV_PAD + ch, W_BR), f32),
                        pltpu.VMEM((H_C, DH_C, DH_C), f32), pltpu.VMEM((8, DH_C), f32), pltpu.VMEM((8, LANES), f32)],
        compiler_params=pltpu.CompilerParams(
            dimension_semantics=("arbitrary", "arbitrary"), vmem_limit_bytes=VMEM_LIMIT),
        name="mlstm",
    )(pad(cq), pad(ck), pad(cv), pad(co), pad(cz), i_pre, f_pre,
      rows8(i_pre.transpose(0, 2, 1)), rows8(f_pre.transpose(0, 2, 1)),
      w_conv[:, :W_BR], w_conv[:, W_BR:], g_norm.reshape(1, -1), tril, conv_hist, c0, rows8(n0),
      rows8(jnp.broadcast_to(m0[:, :, None], (b, H_C, LANES))))
    return y[:, :t], conv_s, c_f, n_f[:, :H_C], m_f[:, :H_C, 0]


def _merge_kernel(ys_ref, gate_ref, wb_ref, wo_ref, x_ref, g_ref, b_ref, o_ref, acc_ref):
    f32, bf16 = jnp.float32, jnp.bfloat16
    n = pl.program_id(1)
    proj = jnp.dot(ys_ref[0].astype(bf16), wb_ref[0], preferred_element_type=f32)
    term = proj / (1.0 + jnp.exp(-gate_ref[...]))

    @pl.when(n == 0)
    def _():
        acc_ref[...] = term

    @pl.when(n > 0)
    def _():
        acc_ref[...] += term

    @pl.when(n == N_BRANCH - 1)
    def _():
        out = jnp.dot(acc_ref[...].astype(bf16), wo_ref[...], preferred_element_type=f32)
        xf = DN_ALPHA * x_ref[...] + out
        xc = xf - jnp.mean(xf, axis=1, keepdims=True)
        var = jnp.mean(xc * xc, axis=1, keepdims=True)
        o_ref[...] = xc * lax.rsqrt(var + LN_EPS) * g_ref[...] + b_ref[...]


def _merge_pallas(ys, gate, w_branch, w_out, x, ln_g, ln_b, tm):
    m = x.shape[0]
    return pl.pallas_call(
        _merge_kernel,
        grid=(m // tm, N_BRANCH),
        in_specs=[pl.BlockSpec((1, tm, W_BR), lambda i, n: (n, i, 0)),
                  pl.BlockSpec((tm, D_MODEL), lambda i, n: (i, n)),
                  pl.BlockSpec((1, W_BR, D_MODEL), lambda i, n: (n, 0, 0)),
                  pl.BlockSpec((D_MODEL, D_MODEL), lambda i, n: (0, 0)),
                  pl.BlockSpec((tm, D_MODEL), lambda i, n: (i, 0)),
                  pl.BlockSpec((1, D_MODEL), lambda i, n: (0, 0)),
                  pl.BlockSpec((1, D_MODEL), lambda i, n: (0, 0))],
        out_specs=pl.BlockSpec((tm, D_MODEL), lambda i, n: (i, 0)),
        out_shape=jax.ShapeDtypeStruct((m, D_MODEL), jnp.float32),
        scratch_shapes=[pltpu.VMEM((tm, D_MODEL), jnp.float32)],
        compiler_params=pltpu.CompilerParams(
            dimension_semantics=("arbitrary", "arbitrary"), vmem_limit_bytes=VMEM_LIMIT),
        name="merge",
    )(ys, gate, w_branch, w_out, x, ln_g.reshape(1, -1), ln_b.reshape(1, -1))


def _split_in(u):
    parts = {}
    off = 0
    for name, width in IN_WIDTHS:
        parts[name] = u[..., off:off + width]
        off += width
    return parts


def _rope(x, pos):
    half = x.shape[-1] // 2
    inv = ROPE_THETA ** (-jnp.arange(half, dtype=jnp.float32) / half)
    ang = pos.astype(jnp.float32)[:, None] * inv[None, :]
    cos = jnp.cos(ang)[:, None, :]
    sin = jnp.sin(ang)[:, None, :]
    x1 = x[..., :half].astype(jnp.float32)
    x2 = x[..., half:].astype(jnp.float32)
    return jnp.concatenate([x1 * cos - x2 * sin, x1 * sin + x2 * cos], -1).astype(x.dtype)


def _head_rms(h, g):
    hf = h.astype(jnp.float32)
    hf = hf * lax.rsqrt(jnp.mean(hf * hf, -1, keepdims=True) + NORM_EPS)
    return (hf.reshape(h.shape[:-2] + (-1,)) * g).astype(h.dtype)


def _layernorm(x, g, b):
    xf = x.astype(jnp.float32)
    xc = xf - jnp.mean(xf, -1, keepdims=True)
    var = jnp.mean(xc * xc, -1, keepdims=True)
    return (xc * lax.rsqrt(var + LN_EPS) * g + b).astype(x.dtype)


def _causal_conv(xs, buf, w):
    t = xs.shape[1]
    full = jnp.concatenate([buf.astype(xs.dtype), xs], 1)
    out = full[:, 0:t] * w[0]
    for j in range(1, CONV_W):
        out = out + full[:, j:j + t] * w[j]
    return out, full[:, t:]


def _gla(q, k, v, log_a, s0):
    b, t, h, dk = q.shape
    dt = v.dtype
    c = math.gcd(t, GLA_CHUNK)
    n = t // c
    f32 = jnp.float32
    def chunks(z):
        return z.astype(f32).reshape(b, n, c, h, -1).transpose(1, 0, 3, 2, 4)
    mask = jnp.tril(jnp.ones((c, c), bool))
    def step(s, inp):
        qb, kb, vb, ab = inp
        cb = jnp.cumsum(ab, axis=2)
        o_inter = jnp.einsum('bhtk,bhkv->bhtv', qb * jnp.exp(cb), s)
        diff = cb[:, :, :, None, :] - cb[:, :, None, :, :]
        decay = jnp.exp(jnp.where(mask[:, :, None], diff, -jnp.inf))
        att = jnp.einsum('bhtk,bhsk,bhtsk->bhts', qb, kb, decay)
        o = o_inter + att @ vb
        c_last = cb[:, :, -1:, :]
        s_new = jnp.exp(c_last[:, :, 0, :])[..., None] * s + jnp.einsum('bhsk,bhsv->bhkv', kb * jnp.exp(c_last - cb), vb)
        return s_new, o
    s_fin, o = lax.scan(step, s0.astype(f32), (chunks(q * (dk ** -0.5)), chunks(k), chunks(v), chunks(log_a)))
    o = o.transpose(1, 0, 3, 2, 4).reshape(b, t, h, -1)
    return o.astype(dt), s_fin.astype(dt)


def _mlstm(q, k, v, i_pre, f_pre, c0, n0, m0):
    b, t, h, d = q.shape
    dt = v.dtype
    f32 = jnp.float32
    c = math.gcd(t, MLSTM_CHUNK)
    n = t // c
    def chunks(z):
        return z.astype(f32).reshape(b, n, c, h, -1).transpose(1, 0, 3, 2, 4)
    def gchunks(z):
        return z.astype(f32).reshape(b, n, c, h).transpose(1, 0, 3, 2)
    mask = jnp.tril(jnp.ones((c, c), bool))
    def step(carry, inp):
        cm, nv, m = carry
        qb, kb, vb, ib, fb = inp
        fcum = jnp.cumsum(jax.nn.log_sigmoid(fb), -1)
        dmat = jnp.where(mask, fcum[..., :, None] - fcum[..., None, :] + ib[..., None, :], -jnp.inf)
        inter = fcum + m[..., None]
        m_t = jnp.maximum(inter, jnp.max(dmat, -1))
        w_inter = jnp.exp(inter - m_t)
        qk = jnp.einsum('bhtd,bhsd->bhts', qb, kb) * jnp.exp(dmat - m_t[..., None])
        num = w_inter[..., None] * jnp.einsum('bhtd,bhde->bhte', qb, cm) + qk @ vb
        den = w_inter * jnp.einsum('bhtd,bhd->bht', qb, nv) + jnp.sum(qk, -1)
        hv = num / jnp.maximum(jnp.abs(den), jnp.exp(-m_t))[..., None]
        f_last = fcum[..., -1]
        g_s = f_last[..., None] - fcum + ib
        m_new = jnp.maximum(f_last + m, jnp.max(g_s, -1))
        a = jnp.exp(f_last + m - m_new)
        ws = jnp.exp(g_s - m_new[..., None])
        c_new = a[..., None, None] * cm + jnp.einsum('bhs,bhsd,bhse->bhde', ws, kb, vb)
        n_new = a[..., None] * nv + jnp.einsum('bhs,bhsd->bhd', ws, kb)
        return (c_new, n_new, m_new), hv
    (c_f, n_f, m_f), hs = lax.scan(step, (c0.astype(f32), n0.astype(f32), m0.astype(f32)),
                                   (chunks(q), chunks(k), chunks(v), gchunks(i_pre), gchunks(f_pre)))
    hs = hs.transpose(1, 0, 3, 2, 4).reshape(b, t, h, d)
    return hs.astype(dt), c_f.astype(dt), n_f.astype(dt), m_f.astype(dt)


def _diff_attn_prompt(q, k, v, lam):
    b, s, h, _, d = q.shape
    nb = s // Q_BLOCK
    qb = q.reshape(b, nb, Q_BLOCK, h, 2, d).swapaxes(0, 1)
    kpos = jnp.arange(s)
    def one_block(args):
        qi, i = args
        qpos = i * Q_BLOCK + jnp.arange(Q_BLOCK)
        sc = jnp.einsum('bthmd,bshmd->bhmts', qi, k).astype(jnp.float32) * (d ** -0.5)
        sc = jnp.where(kpos[None, :] <= qpos[:, None], sc, -jnp.inf)
        p = jax.nn.softmax(sc, -1)
        a = (p[:, :, 0] - lam * p[:, :, 1]).astype(v.dtype)
        return jnp.einsum('bhts,bshe->bthe', a, v)
    o = lax.map(one_block, (qb, jnp.arange(nb)))
    return o.swapaxes(0, 1).reshape(b, s, h, 2 * d)


def _diff_attn_sample(q, k, v, k_past, v_past, lam):
    t = q.shape[1]
    d = q.shape[-1]
    p_len = k_past.shape[1]
    causal = jnp.tril(jnp.ones((t, t), bool))
    s_p = jnp.einsum('bthmd,bshmd->bhmts', q, k_past).astype(jnp.float32)
    s_n = jnp.where(causal, jnp.einsum('bthmd,bshmd->bhmts', q, k).astype(jnp.float32), -jnp.inf)
    p = jax.nn.softmax(jnp.concatenate([s_p, s_n], -1) * (d ** -0.5), -1)
    a = (p[:, :, 0] - lam * p[:, :, 1]).astype(v.dtype)
    return jnp.einsum('bhts,bshe->bthe', a[..., :p_len], v_past) + jnp.einsum('bhts,bshe->bthe', a[..., p_len:], v)


def _index_scores(qi, ki, wi):
    sc = jnp.einsum('bthi,bsi->bths', qi, ki).astype(jnp.float32) * (D_I ** -0.5)
    return jnp.einsum('bths,bth->bts', jax.nn.relu(sc), wi.astype(jnp.float32))


def _dsa_prompt(q, k, v, qi, ki, wi):
    b, s, h, dh = q.shape
    n_sel = min(TOPK_MAX, s // 4)
    nb = s // Q_BLOCK
    def blk(z):
        return z.reshape((b, nb, Q_BLOCK) + z.shape[2:]).swapaxes(0, 1)
    kpos = jnp.arange(s)
    gather = jax.vmap(lambda rows, sel: rows[sel])
    def one_block(args):
        qb, qib, wib, i = args
        qpos = i * Q_BLOCK + jnp.arange(Q_BLOCK)
        score = jnp.where(kpos[None, :] <= qpos[:, None], _index_scores(qib, ki, wib), -jnp.inf)
        _, sel = lax.top_k(score, n_sel)
        kg = gather(k, sel)
        vg = gather(v, sel)
        att = jnp.einsum('bthd,btkhd->bhtk', qb, kg).astype(jnp.float32) * (dh ** -0.5)
        valid = sel <= qpos[None, :, None]
        p = jax.nn.softmax(jnp.where(valid[:, None], att, -jnp.inf), -1)
        return jnp.einsum('bhtk,btkhd->bthd', p.astype(v.dtype), vg)
    o = lax.map(one_block, (blk(q), blk(qi), blk(wi), jnp.arange(nb)))
    return o.swapaxes(0, 1).reshape(b, s, h, dh)


def _dsa_sample(q, k, v, qi, ki, wi, ki_past, pool_k, pool_v, page_table):
    b, t, h, dh = q.shape
    p_len = ki_past.shape[1]
    n_sel = min(TOPK_MAX, (p_len + t) // 4)
    causal = jnp.tril(jnp.ones((t, t), bool))
    score = jnp.concatenate([_index_scores(qi, ki_past, wi),
                             jnp.where(causal, _index_scores(qi, ki, wi), -jnp.inf)], -1)
    _, sel = lax.top_k(score, n_sel)
    ps = jnp.minimum(sel, p_len - 1)
    phys = jnp.take_along_axis(page_table, (ps // PAGE_SIZE).reshape(b, -1), axis=1).reshape(sel.shape)
    flat = phys * PAGE_SIZE + ps % PAGE_SIZE
    ns = jnp.clip(sel - p_len, 0, t - 1)
    gather = jax.vmap(lambda rows, idx: rows[idx])
    is_past = (sel < p_len)[..., None, None]
    kg = jnp.where(is_past, pool_k.reshape((-1, h, dh))[flat], gather(k, ns))
    vg = jnp.where(is_past, pool_v.reshape((-1, h, dh))[flat], gather(v, ns))
    att = jnp.einsum('bthd,btkhd->bhtk', q, kg).astype(jnp.float32) * (dh ** -0.5)
    valid = sel <= (p_len + jnp.arange(t))[None, :, None]
    p = jax.nn.softmax(jnp.where(valid[:, None], att, -jnp.inf), -1)
    return jnp.einsum('bhtk,btkhd->bthd', p.astype(v.dtype), vg)


def _sublayer(x, pos, l, w_in_p, w_a_gate, b_a_gate, g_a, lam_qk, g_b, b_c_if, w_c_conv, g_c,
              w_branch, w_out, ln_g, ln_b, rec, paged):
    b, t, _ = x.shape
    f32 = jnp.float32
    a_s0, c_c0, c_n0, c_m0, c_conv0 = rec
    m = b * t
    w_main, w_gate = w_in_p
    xb = x.reshape(m, D_MODEL).astype(jnp.bfloat16)
    u = _split_in(_matmul(xb, w_main[l], min(m, 512), N_MAIN_PAD // 4).reshape(b, t, N_MAIN_PAD))
    gate = _matmul(xb, w_gate[l], min(m, 512), D_MODEL)
    long_seq = t >= 512
    ya, a_s = _gla_pallas(u['a_q'], u['a_k'], u['a_v'], u['a_g'], u['a_z'], w_a_gate[l], b_a_gate[l], g_a[l], a_s0,
                          512 if long_seq else GLA_CHUNK)
    qb = _rope(u['b_q'].reshape(b, t, 2 * H_B, DH_B), pos).reshape(b, t, H_B, 2, DH_B)
    kb = _rope(u['b_k'].reshape(b, t, 2 * H_B, DH_B), pos).reshape(b, t, H_B, 2, DH_B)
    vb = u['b_v'].reshape(b, t, H_B, 2 * DH_B)
    lam_init = 0.8 - 0.6 * math.exp(-0.3 * l)
    lq = lam_qk[l].astype(f32)
    lam = jnp.exp(jnp.sum(lq[0] * lq[1])) - jnp.exp(jnp.sum(lq[2] * lq[3])) + lam_init
    if paged is None:
        ob = _diff_prompt_pallas(qb.reshape(b, t, W_BR), kb.reshape(b, t, W_BR), vb.reshape(b, t, W_BR),
                                 lam).reshape(b, t, H_B, 2 * DH_B)
    else:
        page_table, pb_k, pb_v, pd_k, pd_v, pd_ki = paged
        ob = _diff_sample_pallas(l, qb, kb, vb, pb_k, pb_v, page_table, lam)
    yb = _head_rms(ob, g_b[l]) * (1.0 - lam_init) * jax.nn.silu(u['b_z'])
    i_pre = u['c_i'].astype(f32) + b_c_if[l, 0]
    f_pre = u['c_f'].astype(f32) + b_c_if[l, 1]
    yc, c_conv, c_c, c_n, c_m = _mlstm_pallas(u['c_q'], u['c_k'], u['c_v'], u['c_o'], u['c_z'], i_pre, f_pre,
                                              w_c_conv[l], g_c[l], c_conv0, c_c0, c_n0, c_m0,
                                              128 if long_seq else TPAD)
    qd = _rope(u['d_q'].reshape(b, t, H_D, DH_D), pos)
    kd = _rope(u['d_k'].reshape(b, t, H_D, DH_D), pos)
    vd = u['d_v'].reshape(b, t, H_D, DH_D)
    qi = _rope(u['d_qi'].reshape(b, t, H_I, D_I), pos)
    ki = _rope(u['d_ki'].reshape(b, t, 1, D_I), pos)[:, :, 0]
    wi = u['d_w'] * (H_I ** -0.5)
    if paged is None:
        od = _dsa_prompt_pallas(qd.reshape(b, t, W_BR), kd.reshape(b, t, W_BR), vd.reshape(b, t, W_BR),
                                qi.reshape(b, t, H_I * D_I), ki, u['d_w'], min(TOPK_MAX, t // 4))
    else:
        od = _dsa_sample_pallas(l, qd, kd, vd, qi, ki, u['d_w'], pd_ki, pd_k, pd_v, page_table)
    yd = od.reshape(b, t, W_BR) * jax.nn.silu(u['d_z'])
    ys = jnp.stack([ya, yb, yc, yd], 0).reshape(N_BRANCH, m, W_BR)
    new_x = _merge_pallas(ys, gate, w_branch[l], w_out[l], x.reshape(m, D_MODEL), ln_g[l], ln_b[l], min(m, 256))
    return new_x.reshape(b, t, D_MODEL), (kb, vb, kd, vd, ki, a_s, c_c, c_n, c_m, c_conv)


def kernel(x_prompt, x_sample, cache_b_k, cache_b_v, cache_d_k, cache_d_v, cache_d_ki,
           state_a_s, state_c_c, state_c_n, state_c_m, state_c_conv, page_table,
           w_in, w_a_gate, b_a_gate, g_a, lam_qk, g_b, b_c_if, w_c_conv, g_c,
           w_branch, w_out, ln_g, ln_b):
    bp, tp, _ = x_prompt.shape
    bd, td, _ = x_sample.shape
    past = page_table.shape[1] * PAGE_SIZE
    pos_p = jnp.arange(tp)
    pos_s = past + jnp.arange(td)
    dt = x_prompt.dtype
    bf16 = jnp.bfloat16
    w_in_p = (jnp.pad(w_in[:, :, :N_MAIN].astype(bf16), ((0, 0), (0, 0), (0, N_MAIN_PAD - N_MAIN))),
              w_in[:, :, N_MAIN:].astype(bf16))
    w_branch = w_branch.astype(bf16)
    w_out = w_out.astype(bf16)
    hp = x_prompt
    hs = x_sample
    new_p = []
    new_s = []
    for l in range(DEPTH):
        rec_p = (jnp.zeros((bp, H_A, DK_A, DV_A), dt), jnp.zeros((bp, H_C, DH_C, DH_C), dt),
                 jnp.zeros((bp, H_C, DH_C), dt), jnp.zeros((bp, H_C), dt),
                 jnp.zeros((bp, CONV_W - 1, 2 * W_BR), dt))
        hp, st_p = _sublayer(hp, pos_p, l, w_in_p, w_a_gate, b_a_gate, g_a, lam_qk, g_b, b_c_if,
                             w_c_conv, g_c, w_branch, w_out, ln_g, ln_b, rec_p, None)
        rec_s = (state_a_s[l], state_c_c[l], state_c_n[l], state_c_m[l], state_c_conv[l])
        paged = (page_table, cache_b_k, cache_b_v, cache_d_k, cache_d_v, cache_d_ki)
        hs, st_s = _sublayer(hs, pos_s, l, w_in_p, w_a_gate, b_a_gate, g_a, lam_qk, g_b, b_c_if,
                             w_c_conv, g_c, w_branch, w_out, ln_g, ln_b, rec_s, paged)
        new_p.append(st_p)
        new_s.append(st_s)
    outs_p = [jnp.stack([st[i] for st in new_p], 0) for i in range(10)]
    outs_s = [jnp.stack([st[i] for st in new_s], 0) for i in range(10)]
    return (hp, hs, *outs_p, *outs_s)
```

```python
import functools
import math
import jax
import jax.numpy as jnp
from jax import lax
import numpy as np
from jax.experimental import pallas as pl
from jax.experimental.pallas import tpu as pltpu

D_MODEL = 2048
DEPTH = 2
PAGE_SIZE = 128
N_BRANCH = 4
W_BR = D_MODEL // 4
H_A = 4
DK_A = W_BR // (2 * H_A)
DV_A = W_BR // H_A
GATE_RANK = 16
GATE_TAU = 16.0
GLA_CHUNK = 16
H_B = 4
DH_B = W_BR // (2 * H_B)
H_C = 4
DH_C = W_BR // H_C
CONV_W = 4
H_D = 4
DH_D = W_BR // H_D
H_I = 4
D_I = 64
TOPK_MAX = 256
ROPE_THETA = 10000.0
LN_EPS = 1e-5
NORM_EPS = 1e-6
DN_ALPHA = (2.0 * DEPTH) ** 0.25

IN_WIDTHS = (
    ('a_q', H_A * DK_A), ('a_k', H_A * DK_A), ('a_v', W_BR), ('a_g', GATE_RANK), ('a_z', W_BR),
    ('b_q', W_BR), ('b_k', W_BR), ('b_v', W_BR), ('b_z', W_BR),
    ('c_q', W_BR), ('c_k', W_BR), ('c_v', W_BR), ('c_i', H_C), ('c_f', H_C), ('c_o', W_BR), ('c_z', W_BR),
    ('d_q', W_BR), ('d_k', W_BR), ('d_v', W_BR), ('d_qi', H_I * D_I), ('d_ki', D_I), ('d_w', H_I), ('d_z', W_BR),
    ('gate', N_BRANCH * D_MODEL),
)
N_IN = sum(w for _, w in IN_WIDTHS)
N_MAIN = N_IN - N_BRANCH * D_MODEL
LANES = 128
N_MAIN_PAD = -(-N_MAIN // (4 * LANES)) * (4 * LANES)


def _mm_kernel(x_ref, w_ref, o_ref):
    o_ref[...] = jnp.dot(x_ref[...], w_ref[...], preferred_element_type=jnp.float32)


def _matmul(x, w, tm, tn):
    m, k = x.shape
    n = w.shape[1]
    return pl.pallas_call(
        _mm_kernel,
        grid=(n // tn, m // tm),
        in_specs=[pl.BlockSpec((tm, k), lambda j, i: (i, 0)),
                  pl.BlockSpec((k, tn), lambda j, i: (0, j))],
        out_specs=pl.BlockSpec((tm, tn), lambda j, i: (i, j)),
        out_shape=jax.ShapeDtypeStruct((m, n), jnp.float32),
        compiler_params=pltpu.CompilerParams(
            dimension_semantics=("arbitrary", "arbitrary"), vmem_limit_bytes=48 * 1024 * 1024),
        name="in_proj",
    )(x, w)


NEG_BIG = -1e30
INT_MIN = -2 ** 31
_NT = (((1,), (1,)), ((), ()))
VMEM_LIMIT = 48 * 1024 * 1024


def _split3(x):
    hi = x.astype(jnp.bfloat16)
    lo = (x - hi.astype(jnp.float32)).astype(jnp.bfloat16)
    return hi, lo


def _flash_consume(s, vh, carry):
    m, l, acc = carry
    m_new = jnp.maximum(m, jnp.max(s, axis=1, keepdims=True))
    alpha = jnp.exp(m - m_new)
    p = jnp.exp(s - m_new)
    l = alpha * l + jnp.sum(p, axis=1, keepdims=True)
    acc = alpha * acc + jnp.dot(p.astype(jnp.bfloat16), vh, preferred_element_type=jnp.float32)
    return m_new, l, acc


def _flash_consume_t(s_t, v_t, carry):
    m, l, acc = carry
    m_new = jnp.maximum(m, jnp.max(s_t, axis=0, keepdims=True))
    alpha = jnp.exp(m - m_new)
    p = jnp.exp(s_t - m_new)
    l = alpha * l + jnp.sum(p, axis=0, keepdims=True)
    acc = alpha * acc + jnp.dot(v_t, p.astype(jnp.bfloat16), preferred_element_type=jnp.float32)
    return m_new, l, acc


def _flash_init_t(tq, dv):
    return (jnp.full((1, tq), NEG_BIG, jnp.float32), jnp.zeros((1, tq), jnp.float32),
            jnp.zeros((dv, tq), jnp.float32))


def _chunked_t(v, kc):
    b, t, w = v.shape
    return v.astype(jnp.bfloat16).reshape(b, t // kc, kc, w).transpose(0, 1, 3, 2)


def _flash_init(tq, dv):
    return (jnp.full((tq, 1), NEG_BIG, jnp.float32), jnp.zeros((tq, 1), jnp.float32),
            jnp.zeros((tq, dv), jnp.float32))


SEARCH_ROWS = 128


def _kth_largest(key_ref, nc, n_sel):
    f32 = jnp.float32
    _, rows, kc = key_ref.shape
    step = min(SEARCH_ROWS, rows)
    assert rows % step == 0

    def search(r0):
        rs = pl.ds(r0, step)

        def count(pred, ref_val):
            def body(c, part):
                hit = jnp.where(pred(key_ref[c, rs, :], ref_val), 1.0, 0.0)
                for j in range(kc // LANES):
                    part = part + hit[:, j * LANES:(j + 1) * LANES]
                return part
            part = lax.fori_loop(0, nc, body, jnp.zeros((step, LANES), f32))
            return jnp.sum(part, axis=1, keepdims=True)

        def bit_body(it, tau):
            cand = tau ^ jnp.left_shift(jnp.int32(1), 31 - it)
            return jnp.where(count(lambda a, b: a >= b, cand) >= n_sel, cand, tau)
        tau = lax.fori_loop(0, 32, bit_body, jnp.full((step, 1), INT_MIN, jnp.int32))
        return tau, n_sel - count(lambda a, b: a > b, tau)
    parts = [search(r0) for r0 in range(0, rows, step)]
    return (jnp.concatenate([p[0] for p in parts], axis=0), jnp.concatenate([p[1] for p in parts], axis=0))


def _dsa_prompt_kernel(q_ref, k_ref, vt_ref, qi_ref, ki_ref, wi_ref, tri_ref, o_ref, key_ref, bias_ref,
                       *, tq, kc, n_sel, scale):
    f32 = jnp.float32
    i = pl.program_id(1)
    nc = (i * tq + tq + kc - 1) // kc
    qpos = i * tq + lax.broadcasted_iota(jnp.int32, (tq, 1), 0)
    lane = lax.broadcasted_iota(jnp.int32, (1, kc), 1)
    wi = wi_ref[0] * (H_I ** -0.5 * D_I ** -0.5)
    wis = [jnp.broadcast_to(wi[:, h:h + 1], (tq, kc)) for h in range(H_I)]
    d3 = 3 * D_I

    def score_body(c, carry):
        off = pl.multiple_of(c * kc, kc)
        kik = ki_ref[0, pl.ds(off, kc), :]
        acc = jnp.zeros((tq, kc), f32)
        for h in range(H_I):
            s = lax.dot_general(qi_ref[0, :, h * d3:(h + 1) * d3], kik, _NT, preferred_element_type=f32)
            acc = acc + wis[h] * jnp.maximum(s, 0.0)
        acc = jnp.where(off + lane <= qpos, acc, -jnp.inf)
        bits = pltpu.bitcast(acc, jnp.int32)
        bits = jnp.where(bits == INT_MIN, 0, bits)
        key_ref[c] = jnp.where(bits < 0, bits ^ 0x7FFFFFFF, bits)
        return carry
    lax.fori_loop(0, nc, score_body, 0)

    tau, need = _kth_largest(key_ref, nc, n_sel)

    def bias_body(c, seen):
        key = key_ref[c]
        tie = key == tau
        tie_f = jnp.where(tie, 1.0, 0.0)
        before = seen + jnp.dot(tie_f.astype(jnp.bfloat16), tri_ref[...], preferred_element_type=f32)
        sel = ((key > tau) | (tie & (before < need))) & (c * kc + lane <= qpos)
        bias_ref[c] = jnp.where(sel, 0.0, NEG_BIG).T
        return seen + jnp.sum(tie_f, axis=1, keepdims=True)
    lax.fori_loop(0, nc, bias_body, jnp.zeros((tq, 1), f32))

    heads = [slice(h * DH_D, (h + 1) * DH_D) for h in range(H_D)]
    qs = [(q_ref[0, :, cols] * scale).astype(jnp.bfloat16) for cols in heads]

    def logits(c):
        rows = pl.ds(pl.multiple_of(c * kc, kc), kc)
        return tuple(lax.dot_general(k_ref[0, rows, heads[h]], qs[h], _NT, preferred_element_type=f32)
                     for h in range(H_D))

    def att_body(c, carry):
        s_cur, states = carry
        s_next = logits(jnp.minimum(c + 1, nc - 1))
        bias = bias_ref[c]
        return s_next, tuple(_flash_consume_t(s_cur[h] + bias, vt_ref[0, c, heads[h], :], states[h])
                             for h in range(H_D))
    _, fin = lax.fori_loop(0, nc, att_body, (logits(0), tuple(_flash_init_t(tq, DH_D) for _ in range(H_D))))
    for h in range(H_D):
        o_ref[0, heads[h], :] = fin[h][2] / fin[h][1]


def _dsa_prompt_pallas(q, k, v, qi, ki, wi, n_sel, tq=256, kc=256):
    b, t, w = q.shape
    bf16 = jnp.bfloat16
    qh, ql = _split3(qi.reshape(b, t, H_I, D_I))
    kh, kl = _split3(ki)
    qi3 = jnp.concatenate([qh, qh, ql], -1).reshape(b, t, H_I * 3 * D_I)
    ki3 = jnp.concatenate([kh, kl, kh], -1)
    tri = jnp.asarray(np.triu(np.ones((kc, kc), np.float32), 1), bf16)
    kern = functools.partial(_dsa_prompt_kernel, tq=tq, kc=kc, n_sel=n_sel, scale=DH_D ** -0.5)
    blk = lambda bi, i: (bi, i, 0)
    full = lambda bi, i: (bi, 0, 0)
    return pl.pallas_call(
        kern,
        grid=(b, t // tq),
        in_specs=[pl.BlockSpec((1, tq, w), blk), pl.BlockSpec((1, t, w), full),
                  pl.BlockSpec((1, t // kc, w, kc), lambda bi, i: (bi, 0, 0, 0)),
                  pl.BlockSpec((1, tq, H_I * 3 * D_I), blk), pl.BlockSpec((1, t, 3 * D_I), full),
                  pl.BlockSpec((1, tq, H_I), blk), pl.BlockSpec((kc, kc), lambda bi, i: (0, 0))],
        out_specs=pl.BlockSpec((1, w, tq), lambda bi, i: (bi, 0, i)),
        out_shape=jax.ShapeDtypeStruct((b, w, t), jnp.float32),
        scratch_shapes=[pltpu.VMEM((t // kc, tq, kc), jnp.int32), pltpu.VMEM((t // kc, kc, tq), jnp.float32)],
        compiler_params=pltpu.CompilerParams(
            dimension_semantics=("arbitrary", "arbitrary"), vmem_limit_bytes=VMEM_LIMIT),
        name="dsa_prompt",
    )(q, k.astype(bf16), _chunked_t(v, kc), qi3, ki3, wi, tri).transpose(0, 2, 1)


def _diff_prompt_kernel(lam_ref, q_ref, k_ref, vt_ref, o_ref, *, tq, kc, scale):
    i = pl.program_id(1)
    nc = (i * tq + tq + kc - 1) // kc
    qpos = i * tq + lax.broadcasted_iota(jnp.int32, (1, tq), 1)
    kpos = lax.broadcasted_iota(jnp.int32, (kc, 1), 0)
    dcol = lax.broadcasted_iota(jnp.int32, (1, 2 * DH_B), 1)
    lam = lam_ref[0, 0]
    for pair in range(H_B // 2):
        heads = [slice(h * 2 * DH_B, (h + 1) * 2 * DH_B) for h in (2 * pair, 2 * pair + 1)]
        qms = [jnp.where((dcol >= m * DH_B) & (dcol < (m + 1) * DH_B), q_ref[0, :, cols] * scale, 0.0).astype(jnp.bfloat16)
               for cols in heads for m in range(2)]

        def logits(c, heads=heads, qms=qms):
            rows = pl.ds(pl.multiple_of(c * kc, kc), kc)
            return tuple(lax.dot_general(k_ref[0, rows, cols], qms[2 * j + m], _NT, preferred_element_type=jnp.float32)
                         for j, cols in enumerate(heads) for m in range(2))

        def att_body(c, carry, heads=heads, logits=logits):
            s_cur, states = carry
            s_next = logits(jnp.minimum(c + 1, nc - 1))
            bias = jnp.where(c * kc + kpos <= qpos, 0.0, NEG_BIG)
            out = []
            for j, cols in enumerate(heads):
                vh = vt_ref[0, c, cols, :]
                for m in range(2):
                    out.append(_flash_consume_t(s_cur[2 * j + m] + bias, vh, states[2 * j + m]))
            return s_next, tuple(out)
        _, fin = lax.fori_loop(0, nc, att_body, (logits(0), tuple(_flash_init_t(tq, 2 * DH_B) for _ in range(4))))
        for j, cols in enumerate(heads):
            o_ref[0, cols, :] = fin[2 * j][2] / fin[2 * j][1] - lam * (fin[2 * j + 1][2] / fin[2 * j + 1][1])


def _diff_prompt_pallas(q, k, v, lam, tq=256, kc=256):
    b, t, w = q.shape
    bf16 = jnp.bfloat16
    kern = functools.partial(_diff_prompt_kernel, tq=tq, kc=kc, scale=DH_B ** -0.5)
    blk = lambda bi, i: (bi, i, 0)
    full = lambda bi, i: (bi, 0, 0)
    return pl.pallas_call(
        kern,
        grid=(b, t // tq),
        in_specs=[pl.BlockSpec(memory_space=pltpu.SMEM),
                  pl.BlockSpec((1, tq, w), blk), pl.BlockSpec((1, t, w), full),
                  pl.BlockSpec((1, t // kc, w, kc), lambda bi, i: (bi, 0, 0, 0))],
        out_specs=pl.BlockSpec((1, w, tq), lambda bi, i: (bi, 0, i)),
        out_shape=jax.ShapeDtypeStruct((b, w, t), jnp.float32),
        compiler_params=pltpu.CompilerParams(
            dimension_semantics=("arbitrary", "arbitrary"), vmem_limit_bytes=VMEM_LIMIT),
        name="diff_prompt",
    )(lam.reshape(1, 1).astype(jnp.float32), q, k.astype(bf16), _chunked_t(v, kc)).transpose(0, 2, 1)


PPS = 16
TPAD = 8
NEG_INF_KEY = INT_MIN + 0x7FFFFF


def _float_order_key(x):
    bits = pltpu.bitcast(x, jnp.int32)
    bits = jnp.where(bits == INT_MIN, 0, bits)
    return jnp.where(bits < 0, bits ^ 0x7FFFFFFF, bits)


def _page_spec(l, r, rows, n_pages):
    def index(b, j, pt):
        return (l, pt[b * n_pages + j * PPS + r], 0, 0)
    return pl.BlockSpec((1, 1, rows, PAGE_SIZE), index)


def _rows_view(cache):
    return cache.reshape(cache.shape[0], cache.shape[1], cache.shape[2] * cache.shape[3], cache.shape[4])


def _online_softmax_step(state, s_list, pv):
    m_old, l, acc = state
    m_new = m_old
    for s in s_list:
        m_new = jnp.maximum(m_new, jnp.max(s, axis=1, keepdims=True))
    alpha = jnp.exp(m_old - m_new)
    p_list = [jnp.exp(s - m_new) for s in s_list]
    l = alpha * l
    for p in p_list:
        l = l + jnp.sum(p, axis=1, keepdims=True)
    return m_new, l, alpha * acc + pv(p_list)


def _online_softmax_update(m_ref, l_ref, acc_ref, s_list, pv):
    m_ref[...], l_ref[...], acc_ref[...] = _online_softmax_step((m_ref[...], l_ref[...], acc_ref[...]), s_list, pv)


def _sample_score_kernel(pt_ref, qh_ref, ql_ref, w_ref, kn_ref, *rest):
    ki_refs, (o_ref, on_ref) = rest[:PPS], rest[PPS:]
    f32 = jnp.float32
    qh, ql, w = qh_ref[0], ql_ref[0], w_ref[0]

    def scores(kt):
        kh, kl = _split3(kt)
        s = (jnp.dot(qh, kh, preferred_element_type=f32) + jnp.dot(qh, kl, preferred_element_type=f32)
             + jnp.dot(ql, kh, preferred_element_type=f32))
        r = w * jnp.maximum(s, 0.0)
        out = r[0:TPAD]
        for h in range(1, H_I):
            out = out + r[h * TPAD:(h + 1) * TPAD]
        return out
    for r in range(PPS):
        o_ref[0, :, r * PAGE_SIZE:(r + 1) * PAGE_SIZE] = scores(ki_refs[r][0, 0])

    @pl.when(pl.program_id(1) == 0)
    def _():
        t = lax.broadcasted_iota(jnp.int32, (TPAD, PAGE_SIZE), 0)
        j = lax.broadcasted_iota(jnp.int32, (TPAD, PAGE_SIZE), 1)
        on_ref[0] = jnp.where(j <= t, scores(kn_ref[0]), -jnp.inf)


def _sample_scores_pallas(l, qi, ki, wi, cache_ki, page_table):
    b, t = qi.shape[:2]
    n_pages = page_table.shape[1]
    qpad = jnp.zeros((b, H_I, TPAD, D_I), jnp.float32).at[:, :, :t].set(qi.transpose(0, 2, 1, 3))
    qh, ql = _split3(qpad.reshape(b, H_I * TPAD, D_I))
    wcol = jnp.zeros((b, H_I, TPAD), jnp.float32).at[:, :, :t].set(wi.transpose(0, 2, 1))
    wcol = (wcol * (H_I ** -0.5 * D_I ** -0.5)).reshape(b, H_I * TPAD, 1)
    knew = jnp.zeros((b, D_I, PAGE_SIZE), jnp.float32).at[:, :, :t].set(ki.transpose(0, 2, 1))
    row = lambda bi, j, pt: (bi, 0, 0)
    grid_spec = pltpu.PrefetchScalarGridSpec(
        num_scalar_prefetch=1, grid=(b, n_pages // PPS),
        in_specs=[pl.BlockSpec((1, H_I * TPAD, D_I), row), pl.BlockSpec((1, H_I * TPAD, D_I), row),
                  pl.BlockSpec((1, H_I * TPAD, 1), row), pl.BlockSpec((1, D_I, PAGE_SIZE), row)]
                 + [_page_spec(l, r, D_I, n_pages) for r in range(PPS)],
        out_specs=[pl.BlockSpec((1, TPAD, PPS * PAGE_SIZE), lambda bi, j, pt: (bi, 0, j)),
                   pl.BlockSpec((1, TPAD, PAGE_SIZE), row)])
    return pl.pallas_call(
        _sample_score_kernel, grid_spec=grid_spec,
        out_shape=[jax.ShapeDtypeStruct((b, TPAD, n_pages * PAGE_SIZE), jnp.float32),
                   jax.ShapeDtypeStruct((b, TPAD, PAGE_SIZE), jnp.float32)],
        compiler_params=pltpu.CompilerParams(dimension_semantics=("arbitrary", "arbitrary")),
        name="dsa_sample_scores",
    )(page_table.reshape(-1), qh, ql, wcol, knew, *([cache_ki.transpose(0, 1, 3, 2)] * PPS))


def _topk_bias_kernel(sc_ref, tri_ref, bias_ref, key_ref, *, n_sel):
    f32 = jnp.float32
    nc, rows, kc = sc_ref.shape

    def key_body(c, carry):
        key_ref[c] = _float_order_key(sc_ref[c])
        return carry
    lax.fori_loop(0, nc, key_body, 0)

    tau, need = _kth_largest(key_ref, nc, n_sel)

    def bias_body(c, seen):
        key = key_ref[c]
        tie = key == tau
        tie_f = jnp.where(tie, 1.0, 0.0)
        before = seen + jnp.dot(tie_f.astype(jnp.bfloat16), tri_ref[...], preferred_element_type=f32)
        sel = ((key > tau) | (tie & (before < need))) & (key > NEG_INF_KEY)
        bias_ref[c] = jnp.where(sel, 0.0, NEG_BIG)
        return seen + jnp.sum(tie_f, axis=1, keepdims=True)
    lax.fori_loop(0, nc, bias_body, jnp.zeros((rows, 1), f32))


def _topk_bias_pallas(scores, n_sel, kc=256):
    rows, n = scores.shape
    nc = n // kc
    tri = jnp.asarray(np.triu(np.ones((kc, kc), np.float32), 1), jnp.bfloat16)
    bias = pl.pallas_call(
        functools.partial(_topk_bias_kernel, n_sel=n_sel),
        out_shape=jax.ShapeDtypeStruct((nc, rows, kc), jnp.float32),
        scratch_shapes=[pltpu.VMEM((nc, rows, kc), jnp.int32)],
        compiler_params=pltpu.CompilerParams(vmem_limit_bytes=VMEM_LIMIT),
        name="topk_bias",
    )(scores.reshape(rows, nc, kc).transpose(1, 0, 2), tri)
    return bias.transpose(1, 0, 2).reshape(rows, n)


def _dsa_sample_attn_kernel(pt_ref, q_ref, kn_ref, vn_ref, bn_ref, bp_ref, *rest, scale):
    f32, bf16 = jnp.float32, jnp.bfloat16
    k_refs, v_refs = rest[:PPS], rest[PPS:2 * PPS]
    o_ref, m_ref, l_ref, acc_ref = rest[2 * PPS:]
    j = pl.program_id(1)
    wide = PAGE_SIZE * H_D

    @pl.when(j == 0)
    def _():
        m_ref[...] = jnp.full(m_ref.shape, NEG_BIG, f32)
        l_ref[...] = jnp.zeros(l_ref.shape, f32)
        acc_ref[...] = jnp.zeros(acc_ref.shape, f32)

    q = q_ref[0]
    s_list = [lax.dot_general(q, k_refs[r][0, 0].astype(bf16), _NT, preferred_element_type=f32) * scale
              + bp_ref[0, :, r * wide:(r + 1) * wide] for r in range(PPS)]

    def pv(p_list):
        out = jnp.dot(p_list[0].astype(bf16), v_refs[0][0, 0].astype(bf16), preferred_element_type=f32)
        for r in range(1, PPS):
            out = out + jnp.dot(p_list[r].astype(bf16), v_refs[r][0, 0].astype(bf16), preferred_element_type=f32)
        return out
    _online_softmax_update(m_ref, l_ref, acc_ref, s_list, pv)

    @pl.when(j == pl.num_programs(1) - 1)
    def _():
        s = lax.dot_general(q, kn_ref[0], _NT, preferred_element_type=f32) * scale + bn_ref[0]
        _online_softmax_update(m_ref, l_ref, acc_ref, [s],
                               lambda p: jnp.dot(p[0].astype(bf16), vn_ref[0], preferred_element_type=f32))
        o_ref[0] = acc_ref[...] / l_ref[...]


def _pad_tokens(x, fill=0.0):
    b, t = x.shape[:2]
    return jnp.full((b, TPAD) + x.shape[2:], fill, x.dtype).at[:, :t].set(x)


def _dsa_sample_pallas(l, q, k, v, qi, ki, wi_raw, cache_ki, cache_k, cache_v, page_table):
    b, t, h, dh = q.shape
    n_pages = page_table.shape[1]
    p_len = n_pages * PAGE_SIZE
    n_sel = min(TOPK_MAX, (p_len + t) // 4)
    bf16 = jnp.bfloat16
    s_past, s_new = _sample_scores_pallas(l, qi, ki, wi_raw, cache_ki, page_table)
    pad = jnp.full((b, t, PAGE_SIZE), -jnp.inf, jnp.float32)
    scores = jnp.concatenate([s_past[:, :t], s_new[:, :t], pad], -1).reshape(b * t, p_len + 2 * PAGE_SIZE)
    bias = _topk_bias_pallas(scores, n_sel).reshape(b, t, -1)
    rows = h * t
    same_head = jnp.arange(h)[:, None] == jnp.arange(h)[None, :]

    def per_head_bias(bz):
        full = jnp.where(same_head[None, :, None, None, :], bz[:, None, :, :, None], NEG_BIG)
        return full.reshape(b, rows, bz.shape[-1] * h)
    new_rows = TPAD * h
    bias_new = per_head_bias(jnp.full((b, t, TPAD), NEG_BIG, jnp.float32).at[:, :, :t].set(bias[:, :, p_len:p_len + t]))
    key_rows = lambda z: jnp.zeros((b, new_rows, dh), bf16).at[:, :t * h].set(z.reshape(b, t * h, dh).astype(bf16))
    row3 = lambda bi, j, pt: (bi, 0, 0)
    grid_spec = pltpu.PrefetchScalarGridSpec(
        num_scalar_prefetch=1, grid=(b, n_pages // PPS),
        in_specs=[pl.BlockSpec((1, rows, dh), row3), pl.BlockSpec((1, new_rows, dh), row3),
                  pl.BlockSpec((1, new_rows, dh), row3), pl.BlockSpec((1, rows, new_rows), row3),
                  pl.BlockSpec((1, rows, PPS * PAGE_SIZE * h), lambda bi, j, pt: (bi, 0, j))]
                 + [_page_spec(l, r, PAGE_SIZE * h, n_pages) for r in range(PPS)] * 2,
        out_specs=pl.BlockSpec((1, rows, dh), row3),
        scratch_shapes=[pltpu.VMEM((rows, 1), jnp.float32), pltpu.VMEM((rows, 1), jnp.float32),
                        pltpu.VMEM((rows, dh), jnp.float32)])
    res = pl.pallas_call(
        functools.partial(_dsa_sample_attn_kernel, scale=dh ** -0.5), grid_spec=grid_spec,
        out_shape=jax.ShapeDtypeStruct((b, rows, dh), jnp.float32),
        compiler_params=pltpu.CompilerParams(
            dimension_semantics=("arbitrary", "arbitrary"), vmem_limit_bytes=VMEM_LIMIT),
        name="dsa_sample_attn",
    )(page_table.reshape(-1), q.transpose(0, 2, 1, 3).reshape(b, rows, dh).astype(bf16), key_rows(k), key_rows(v),
      bias_new, per_head_bias(bias[:, :, :p_len]), *([_rows_view(cache_k)] * PPS), *([_rows_view(cache_v)] * PPS))
    return res.reshape(b, h, t, dh).transpose(0, 2, 1, 3)


def _diff_sample_attn_kernel(pt_ref, q_ref, knt_ref, vn_ref, bn_ref, *rest, scale):
    f32, bf16 = jnp.float32, jnp.bfloat16
    kt_refs, v_refs = rest[:PPS], rest[PPS:2 * PPS]
    o_ref, m_ref, l_ref, acc_ref = rest[2 * PPS:]
    j = pl.program_id(1)
    rph = 2 * TPAD // 2

    @pl.when(j == 0)
    def _():
        m_ref[...] = jnp.full(m_ref.shape, NEG_BIG, f32)
        l_ref[...] = jnp.zeros(l_ref.shape, f32)
        acc_ref[...] = jnp.zeros(acc_ref.shape, f32)

    q = q_ref[0]
    s_list = [jnp.dot(q, kt_refs[r][0, 0].astype(bf16), preferred_element_type=f32) * scale for r in range(PPS)]

    def pv(p_list):
        outs = []
        for h in range(H_B):
            head = pl.ds(h, PAGE_SIZE, stride=H_B)
            rows = slice(h * rph, (h + 1) * rph)
            out = jnp.dot(p_list[0][rows].astype(bf16), v_refs[0][0, 0, head, :].astype(bf16), preferred_element_type=f32)
            for r in range(1, PPS):
                out = out + jnp.dot(p_list[r][rows].astype(bf16), v_refs[r][0, 0, head, :].astype(bf16),
                                    preferred_element_type=f32)
            outs.append(out)
        return jnp.concatenate(outs, axis=0)
    _online_softmax_update(m_ref, l_ref, acc_ref, s_list, pv)

    @pl.when(j == pl.num_programs(1) - 1)
    def _():
        s = jnp.dot(q, knt_ref[0], preferred_element_type=f32) * scale + bn_ref[0]

        def pv_new(p):
            return jnp.concatenate([jnp.dot(p[0][h * rph:(h + 1) * rph].astype(bf16), vn_ref[0, h],
                                            preferred_element_type=f32) for h in range(H_B)], axis=0)
        _online_softmax_update(m_ref, l_ref, acc_ref, [s], pv_new)
        o_ref[0] = acc_ref[...] / l_ref[...]


def _diff_sample_pallas(l, q, k, v, cache_k, cache_v, page_table, lam):
    b, t, h = q.shape[:3]
    n_pages = page_table.shape[1]
    w = h * 2 * DH_B
    dv = 2 * DH_B
    tp = TPAD // 2
    rows = 2 * h * tp
    bf16 = jnp.bfloat16
    blk = jnp.arange(w) // DH_B
    qpad = jnp.zeros((b, tp, w), jnp.float32).at[:, :t].set(q.reshape(b, t, w))
    qbd = jnp.where((blk[None, :] == jnp.arange(2 * h)[:, None])[None, :, None, :], qpad[:, None], 0.0)
    qbd = qbd.reshape(b, rows, w).astype(bf16)
    knt = jnp.zeros((b, w, TPAD), jnp.float32).at[:, :, :t].set(k.reshape(b, t, w).transpose(0, 2, 1)).astype(bf16)
    vn = _pad_tokens(v).transpose(0, 2, 1, 3).astype(bf16)
    causal = jnp.where(jnp.arange(TPAD)[None, :] <= jnp.arange(tp)[:, None], 0.0, NEG_BIG)
    causal = jnp.where(jnp.arange(TPAD)[None, :] < t, causal, NEG_BIG)
    bias_new = jnp.broadcast_to(jnp.tile(causal, (2 * h, 1))[None], (b, rows, TPAD))
    kt_view = cache_k.transpose(0, 1, 3, 4, 5, 2).reshape(cache_k.shape[0], cache_k.shape[1], w, PAGE_SIZE)
    row3 = lambda bi, j, pt: (bi, 0, 0)
    grid_spec = pltpu.PrefetchScalarGridSpec(
        num_scalar_prefetch=1, grid=(b, n_pages // PPS),
        in_specs=[pl.BlockSpec((1, rows, w), row3), pl.BlockSpec((1, w, TPAD), row3),
                  pl.BlockSpec((1, h, TPAD, dv), lambda bi, j, pt: (bi, 0, 0, 0)), pl.BlockSpec((1, rows, TPAD), row3)]
                 + [_page_spec(l, r, w, n_pages) for r in range(PPS)]
                 + [_page_spec(l, r, PAGE_SIZE * h, n_pages) for r in range(PPS)],
        out_specs=pl.BlockSpec((1, rows, dv), row3),
        scratch_shapes=[pltpu.VMEM((rows, 1), jnp.float32), pltpu.VMEM((rows, 1), jnp.float32),
                        pltpu.VMEM((rows, dv), jnp.float32)])
    res = pl.pallas_call(
        functools.partial(_diff_sample_attn_kernel, scale=DH_B ** -0.5), grid_spec=grid_spec,
        out_shape=jax.ShapeDtypeStruct((b, rows, dv), jnp.float32),
        compiler_params=pltpu.CompilerParams(
            dimension_semantics=("arbitrary", "arbitrary"), vmem_limit_bytes=VMEM_LIMIT),
        name="diff_sample_attn",
    )(page_table.reshape(-1), qbd, knt, vn, bias_new, *([kt_view] * PPS), *([_rows_view(cache_v)] * PPS))
    res = res.reshape(b, h, 2, tp, dv)[:, :, :, :t]
    return (res[:, :, 0] - lam * res[:, :, 1]).transpose(0, 2, 1, 3)


_TN = (((0,), (0,)), ((), ()))


def _silu(x):
    return x / (1.0 + jnp.exp(-x))


def _log_sigmoid(x):
    return jnp.minimum(x, 0.0) - jnp.log(1.0 + jnp.exp(-jnp.abs(x)))


def _head_rms_gate(o, g, z, d):
    outs = []
    for h in range(o.shape[1] // d):
        cols = slice(h * d, (h + 1) * d)
        oh = o[:, cols]
        outs.append(oh * lax.rsqrt(jnp.mean(oh * oh, axis=1, keepdims=True) + NORM_EPS))
    return jnp.concatenate(outs, axis=1) * g * _silu(z)


def _gla_kernel(q_ref, k_ref, v_ref, g_ref, wg_ref, bg_ref, z_ref, gn_ref, tril_ref, s0_ref,
                y_ref, sout_ref, st_ref, la_ref, o_ref, *, tb, t_valid):
    f32, bf16 = jnp.float32, jnp.bfloat16
    c = GLA_CHUNK

    @pl.when(pl.program_id(1) == 0)
    def _():
        st_ref[...] = s0_ref[0]

    gl = jnp.dot(g_ref[0].astype(bf16), wg_ref[...].astype(bf16), preferred_element_type=f32) + bg_ref[...]
    la = _log_sigmoid(gl) / GATE_TAU
    if t_valid < tb:
        la = jnp.where(lax.broadcasted_iota(jnp.int32, (tb, 1), 0) < t_valid, la, 0.0)
    la_ref[...] = la
    trow = lax.broadcasted_iota(jnp.int32, (c, 1), 0)
    lane_id = lax.broadcasted_iota(jnp.int32, (1, LANES), 1)

    def chunk(ci, carry):
        rows = pl.ds(pl.multiple_of(ci * c, c), c)
        cb = jnp.dot(tril_ref[...], la_ref[rows, :], precision=lax.Precision.HIGHEST, preferred_element_type=f32)
        c_last = cb[c - 1:c, :]
        qc = q_ref[0, rows, :] * (DK_A ** -0.5)
        kc = k_ref[0, rows, :]
        vc = v_ref[0, rows, :]
        qe = (qc * jnp.exp(cb)).astype(bf16)
        kdec = (kc * jnp.exp(c_last - cb)).astype(bf16)
        dec = jnp.exp(c_last)
        s_ts = [st_ref[h] for h in range(H_A)]
        intra = [[] for _ in range(H_A)]
        for pair in range(H_A // 2):
            slab = slice(pair * LANES, (pair + 1) * LANES)
            q_p, k_p, cb_p = qc[:, slab], kc[:, slab], cb[:, slab]
            for s in range(c):
                x = q_p * k_p[s:s + 1, :] * jnp.exp(cb_p - cb_p[s:s + 1, :])
                for half in range(2):
                    h = 2 * pair + half
                    col = jnp.sum(jnp.where((lane_id >= half * DK_A) & (lane_id < (half + 1) * DK_A), x, 0.0),
                                  axis=1, keepdims=True)
                    intra[h].append(jnp.where(trow >= s, col, 0.0) * vc[s:s + 1, h * DV_A:(h + 1) * DV_A])
        outs = []
        for h in range(H_A):
            kcols = slice(h * DK_A, (h + 1) * DK_A)
            parts = intra[h]
            while len(parts) > 1:
                parts = [parts[i] + parts[i + 1] for i in range(0, len(parts), 2)]
            outs.append(parts[0] + lax.dot_general(qe[:, kcols], s_ts[h].astype(bf16), _NT, preferred_element_type=f32))
            s_ts[h] = s_ts[h] * dec[:, kcols] + lax.dot_general(
                vc[:, h * DV_A:(h + 1) * DV_A].astype(bf16), kdec[:, kcols], _TN, preferred_element_type=f32)
        o_ref[rows, :] = jnp.concatenate(outs, axis=1)
        for h in range(H_A):
            st_ref[h] = s_ts[h]
        return carry
    lax.fori_loop(0, tb // c, chunk, 0, unroll=2 if tb // c > 1 else 1)
    y_ref[0] = _head_rms_gate(o_ref[...], gn_ref[...], z_ref[0], DV_A)

    @pl.when(pl.program_id(1) == pl.num_programs(1) - 1)
    def _():
        sout_ref[0] = st_ref[...]


def _gla_pallas(q, k, v, g, z, w_gate, b_gate, g_norm, s0, tb):
    b, t, _ = q.shape
    tp = -(-t // tb) * tb
    pad = lambda x: x if tp == t else jnp.zeros((b, tp, x.shape[2]), x.dtype).at[:, :t].set(x)
    tril = jnp.asarray(np.tril(np.ones((GLA_CHUNK, GLA_CHUNK), np.float32)))
    blk = lambda w: pl.BlockSpec((1, tb, w), lambda bi, j: (bi, j, 0))
    const = lambda shape: pl.BlockSpec(shape, lambda bi, j: (0,) * len(shape))
    st_spec = pl.BlockSpec((1, H_A, DV_A, DK_A), lambda bi, j: (bi, 0, 0, 0))
    y, s_t = pl.pallas_call(
        functools.partial(_gla_kernel, tb=tb, t_valid=min(t, tb)),
        grid=(b, tp // tb),
        in_specs=[blk(H_A * DK_A), blk(H_A * DK_A), blk(W_BR), blk(GATE_RANK),
                  const((GATE_RANK, H_A * DK_A)), const((1, H_A * DK_A)), blk(W_BR), const((1, W_BR)),
                  const((GLA_CHUNK, GLA_CHUNK)), st_spec],
        out_specs=[blk(W_BR), st_spec],
        out_shape=[jax.ShapeDtypeStruct((b, tp, W_BR), jnp.float32),
                   jax.ShapeDtypeStruct((b, H_A, DV_A, DK_A), jnp.float32)],
        scratch_shapes=[pltpu.VMEM((H_A, DV_A, DK_A), jnp.float32), pltpu.VMEM((tb, H_A * DK_A), jnp.float32),
                        pltpu.VMEM((tb, W_BR), jnp.float32)],
        compiler_params=pltpu.CompilerParams(
            dimension_semantics=("arbitrary", "arbitrary"), vmem_limit_bytes=VMEM_LIMIT),
        name="gla",
    )(pad(q), pad(k), pad(v), pad(g), w_gate, b_gate.reshape(1, -1), pad(z), g_norm.reshape(1, -1), tril,
      s0.swapaxes(2, 3))
    return y[:, :t], s_t.swapaxes(2, 3)


CONV_PAD = 8


def _mlstm_kernel(cq_ref, ck_ref, cv_ref, co_ref, cz_ref, icol_ref, fcol_ref, irow_ref, frow_ref,
                  wq_ref, wk_ref, gn_ref, tril_ref, conv0_ref, c0_ref, n0_ref, m0_ref,
                  y_ref, conv_ref, c_ref, n_ref, m_ref, xq_ref, xk_ref, cm_ref, nv_ref, mv_ref, *, ch, t_valid):
    f32, bf16 = jnp.float32, jnp.bfloat16
    hi = lax.Precision.HIGHEST
    first = CONV_PAD - (CONV_W - 1)

    @pl.when(pl.program_id(1) == 0)
    def _():
        xq_ref[0:CONV_PAD, :] = conv0_ref[0, :, 0:W_BR]
        xk_ref[0:CONV_PAD, :] = conv0_ref[0, :, W_BR:2 * W_BR]
        cm_ref[...] = c0_ref[0]
        nv_ref[...] = n0_ref[0]
        mv_ref[...] = m0_ref[0]

    def conv(x_ref, new_ref, w_ref, cols):
        x_ref[CONV_PAD:CONV_PAD + ch, :] = new_ref[0]
        out = x_ref[first:first + ch, :] * w_ref[0:1, :]
        for j in range(1, CONV_W):
            out = out + x_ref[first + j:first + j + ch, :] * w_ref[j:j + 1, :]
        conv_ref[0, :, cols] = x_ref[first + t_valid:first + t_valid + CONV_W - 1, :]
        tail = x_ref[ch:ch + CONV_PAD, :]
        x_ref[0:CONV_PAD, :] = tail
        return _silu(out)
    qc = conv(xq_ref, cq_ref, wq_ref, slice(0, W_BR))
    kc = conv(xk_ref, ck_ref, wk_ref, slice(W_BR, 2 * W_BR)) * (DH_C ** -0.5)

    tril = tril_ref[...]
    fcum_c = jnp.dot(tril, _log_sigmoid(fcol_ref[0]), precision=hi, preferred_element_type=f32)
    fcum_r = lax.dot_general(_log_sigmoid(frow_ref[0]), tril, _NT, precision=hi, preferred_element_type=f32)
    i_c, i_r = icol_ref[0], irow_ref[0]
    causal = lax.broadcasted_iota(jnp.int32, (ch, ch), 1) <= lax.broadcasted_iota(jnp.int32, (ch, ch), 0)
    outs = []
    for h in range(H_C):
        cols = slice(h * DH_C, (h + 1) * DH_C)
        q_h, k_h, v_h = qc[:, cols], kc[:, cols], cv_ref[0, :, cols]
        cm, nv, m_prev = cm_ref[h], nv_ref[h:h + 1, :], mv_ref[h:h + 1, 0:1]
        fc, fr, ic, ir = fcum_c[:, h:h + 1], fcum_r[h:h + 1, :], i_c[:, h:h + 1], i_r[h:h + 1, :]
        dmat = jnp.where(causal, fc - fr + ir, NEG_BIG)
        inter = fc + m_prev
        m_t = jnp.maximum(inter, jnp.max(dmat, axis=1, keepdims=True))
        w_inter = jnp.exp(inter - m_t)
        qk = lax.dot_general(q_h.astype(bf16), k_h.astype(bf16), _NT, preferred_element_type=f32) * jnp.exp(dmat - m_t)
        num = (w_inter * jnp.dot(q_h.astype(bf16), cm.astype(bf16), preferred_element_type=f32)
               + jnp.dot(qk.astype(bf16), v_h.astype(bf16), preferred_element_type=f32))
        den = w_inter * jnp.sum(q_h * nv, axis=1, keepdims=True) + jnp.sum(qk, axis=1, keepdims=True)
        hv = num / jnp.maximum(jnp.abs(den), jnp.exp(-m_t))
        outs.append(hv / (1.0 + jnp.exp(-co_ref[0, :, cols])))
        f_last = fc[ch - 1:ch, :]
        m_new = jnp.maximum(f_last + m_prev, jnp.max(f_last - fr + ir, axis=1, keepdims=True))
        a = jnp.exp(f_last + m_prev - m_new)
        kw = k_h * jnp.exp(f_last - fc + ic - m_new)
        cm_ref[h] = a * cm + lax.dot_general(kw.astype(bf16), v_h.astype(bf16), _TN, preferred_element_type=f32)
        nv_ref[h:h + 1, :] = a * nv + jnp.sum(kw, axis=0, keepdims=True)
        mv_ref[h:h + 1, :] = jnp.broadcast_to(m_new, (1, LANES))
    y_ref[0] = _head_rms_gate(jnp.concatenate(outs, axis=1), gn_ref[...], cz_ref[0], DH_C)

    @pl.when(pl.program_id(1) == pl.num_programs(1) - 1)
    def _():
        c_ref[0] = cm_ref[...]
        n_ref[0] = nv_ref[...]
        m_ref[0] = mv_ref[...]


def _mlstm_pallas(cq, ck, cv, co, cz, i_pre, f_pre, w_conv, g_norm, conv0, c0, n0, m0, ch):
    b, t, _ = cq.shape
    tp = -(-t // ch) * ch
    f32 = jnp.float32
    pad = lambda x, fill=0.0: x if tp == t else jnp.full((b, tp, x.shape[2]), fill, x.dtype).at[:, :t].set(x)
    i_pre, f_pre = pad(i_pre, NEG_BIG), pad(f_pre, -NEG_BIG)
    rows8 = lambda x: jnp.zeros((b, 8, x.shape[2]), f32).at[:, :x.shape[1]].set(x)
    tril = jnp.asarray(np.tril(np.ones((ch, ch), np.float32)))
    conv_hist = jnp.zeros((b, CONV_PAD, 2 * W_BR), f32).at[:, CONV_PAD - (CONV_W - 1):].set(conv0)
    assert tp == t or tp == ch
    blk = lambda w: pl.BlockSpec((1, ch, w), lambda bi, j: (bi, j, 0))
    rowblk = pl.BlockSpec((1, 8, ch), lambda bi, j: (bi, 0, j))
    const = lambda shape: pl.BlockSpec(shape, lambda bi, j: (0,) * len(shape))
    per_b = lambda shape: pl.BlockSpec((1,) + shape, lambda bi, j: (bi,) + (0,) * len(shape))
    y, conv_s, c_f, n_f, m_f = pl.pallas_call(
        functools.partial(_mlstm_kernel, ch=ch, t_valid=t - (tp - ch)),
        grid=(b, tp // ch),
        in_specs=[blk(W_BR)] * 5 + [blk(H_C), blk(H_C), rowblk, rowblk,
                  const((CONV_W, W_BR)), const((CONV_W, W_BR)), const((1, W_BR)), const((ch, ch)),
                  per_b((CONV_PAD, 2 * W_BR)), per_b((H_C, DH_C, DH_C)), per_b((8, DH_C)), per_b((8, LANES))],
        out_specs=[blk(W_BR), per_b((CONV_W - 1, 2 * W_BR)), per_b((H_C, DH_C, DH_C)), per_b((8, DH_C)),
                   per_b((8, LANES))],
        out_shape=[jax.ShapeDtypeStruct((b, tp, W_BR), f32), jax.ShapeDtypeStruct((b, CONV_W - 1, 2 * W_BR), f32),
                   jax.ShapeDtypeStruct((b, H_C, DH_C, DH_C), f32), jax.ShapeDtypeStruct((b, 8, DH_C), f32),
                   jax.ShapeDtypeStruct((b, 8, LANES), f32)],
        scratch_shapes=[pltpu.VMEM((CONV_PAD + ch, W_BR), f32), pltpu.VMEM((CONV_PAD + ch, W_BR), f32),
                        pltpu.VMEM((H_C, DH_C, DH_C), f32), pltpu.VMEM((8, DH_C), f32), pltpu.VMEM((8, LANES), f32)],
        compiler_params=pltpu.CompilerParams(
            dimension_semantics=("arbitrary", "arbitrary"), vmem_limit_bytes=VMEM_LIMIT),
        name="mlstm",
    )(pad(cq), pad(ck), pad(cv), pad(co), pad(cz), i_pre, f_pre,
      rows8(i_pre.transpose(0, 2, 1)), rows8(f_pre.transpose(0, 2, 1)),
      w_conv[:, :W_BR], w_conv[:, W_BR:], g_norm.reshape(1, -1), tril, conv_hist, c0, rows8(n0),
      rows8(jnp.broadcast_to(m0[:, :, None], (b, H_C, LANES))))
    return y[:, :t], conv_s, c_f, n_f[:, :H_C], m_f[:, :H_C, 0]


def _merge_kernel(ys_ref, gate_ref, wb_ref, wo_ref, x_ref, g_ref, b_ref, o_ref, acc_ref):
    f32, bf16 = jnp.float32, jnp.bfloat16
    n = pl.program_id(1)
    proj = jnp.dot(ys_ref[0].astype(bf16), wb_ref[0], preferred_element_type=f32)
    term = proj / (1.0 + jnp.exp(-gate_ref[...]))

    @pl.when(n == 0)
    def _():
        acc_ref[...] = term

    @pl.when(n > 0)
    def _():
        acc_ref[...] += term

    @pl.when(n == N_BRANCH - 1)
    def _():
        out = jnp.dot(acc_ref[...].astype(bf16), wo_ref[...], preferred_element_type=f32)
        xf = DN_ALPHA * x_ref[...] + out
        xc = xf - jnp.mean(xf, axis=1, keepdims=True)
        var = jnp.mean(xc * xc, axis=1, keepdims=True)
        o_ref[...] = xc * lax.rsqrt(var + LN_EPS) * g_ref[...] + b_ref[...]


def _merge_pallas(ys, gate, w_branch, w_out, x, ln_g, ln_b, tm):
    m = x.shape[0]
    return pl.pallas_call(
        _merge_kernel,
        grid=(m // tm, N_BRANCH),
        in_specs=[pl.BlockSpec((1, tm, W_BR), lambda i, n: (n, i, 0)),
                  pl.BlockSpec((tm, D_MODEL), lambda i, n: (i, n)),
                  pl.BlockSpec((1, W_BR, D_MODEL), lambda i, n: (n, 0, 0)),
                  pl.BlockSpec((D_MODEL, D_MODEL), lambda i, n: (0, 0)),
                  pl.BlockSpec((tm, D_MODEL), lambda i, n: (i, 0)),
                  pl.BlockSpec((1, D_MODEL), lambda i, n: (0, 0)),
                  pl.BlockSpec((1, D_MODEL), lambda i, n: (0, 0))],
        out_specs=pl.BlockSpec((tm, D_MODEL), lambda i, n: (i, 0)),
        out_shape=jax.ShapeDtypeStruct((m, D_MODEL), jnp.float32),
        scratch_shapes=[pltpu.VMEM((tm, D_MODEL), jnp.float32)],
        compiler_params=pltpu.CompilerParams(
            dimension_semantics=("arbitrary", "arbitrary"), vmem_limit_bytes=VMEM_LIMIT),
        name="merge",
    )(ys, gate, w_branch, w_out, x, ln_g.reshape(1, -1), ln_b.reshape(1, -1))


def _split_in(u):
    parts = {}
    off = 0
    for name, width in IN_WIDTHS[:-1]:
        parts[name] = u[..., off:off + width]
        off += width
    return parts


def _rope(x, pos):
    half = x.shape[-1] // 2
    inv = ROPE_THETA ** (-jnp.arange(half, dtype=jnp.float32) / half)
    ang = pos.astype(jnp.float32)[:, None] * inv[None, :]
    cos = jnp.cos(ang)[:, None, :]
    sin = jnp.sin(ang)[:, None, :]
    x1 = x[..., :half].astype(jnp.float32)
    x2 = x[..., half:].astype(jnp.float32)
    return jnp.concatenate([x1 * cos - x2 * sin, x1 * sin + x2 * cos], -1).astype(x.dtype)


def _head_rms(h, g):
    hf = h.astype(jnp.float32)
    hf = hf * lax.rsqrt(jnp.mean(hf * hf, -1, keepdims=True) + NORM_EPS)
    return (hf.reshape(h.shape[:-2] + (-1,)) * g).astype(h.dtype)


def _sublayer(x, pos, l, w_in_p, w_a_gate, b_a_gate, g_a, lam_qk, g_b, b_c_if, w_c_conv, g_c,
              w_branch, w_out, ln_g, ln_b, rec, paged):
    b, t, _ = x.shape
    f32 = jnp.float32
    a_s0, c_c0, c_n0, c_m0, c_conv0 = rec
    m = b * t
    w_main, w_gate = w_in_p
    xb = x.reshape(m, D_MODEL).astype(jnp.bfloat16)
    u = _split_in(_matmul(xb, w_main[l], min(m, 512), N_MAIN_PAD // 4).reshape(b, t, N_MAIN_PAD))
    gate = _matmul(xb, w_gate[l], min(m, 512), D_MODEL)
    long_seq = t >= 512
    ya, a_s = _gla_pallas(u['a_q'], u['a_k'], u['a_v'], u['a_g'], u['a_z'], w_a_gate[l], b_a_gate[l], g_a[l], a_s0,
                          512 if long_seq else GLA_CHUNK)
    qb = _rope(u['b_q'].reshape(b, t, 2 * H_B, DH_B), pos).reshape(b, t, H_B, 2, DH_B)
    kb = _rope(u['b_k'].reshape(b, t, 2 * H_B, DH_B), pos).reshape(b, t, H_B, 2, DH_B)
    vb = u['b_v'].reshape(b, t, H_B, 2 * DH_B)
    lam_init = 0.8 - 0.6 * math.exp(-0.3 * l)
    lq = lam_qk[l].astype(f32)
    lam = jnp.exp(jnp.sum(lq[0] * lq[1])) - jnp.exp(jnp.sum(lq[2] * lq[3])) + lam_init
    if paged is None:
        ob = _diff_prompt_pallas(qb.reshape(b, t, W_BR), kb.reshape(b, t, W_BR), vb.reshape(b, t, W_BR),
                                 lam).reshape(b, t, H_B, 2 * DH_B)
    else:
        page_table, pb_k, pb_v, pd_k, pd_v, pd_ki = paged
        ob = _diff_sample_pallas(l, qb, kb, vb, pb_k, pb_v, page_table, lam)
    yb = _head_rms(ob, g_b[l]) * (1.0 - lam_init) * jax.nn.silu(u['b_z'])
    i_pre = u['c_i'].astype(f32) + b_c_if[l, 0]
    f_pre = u['c_f'].astype(f32) + b_c_if[l, 1]
    yc, c_conv, c_c, c_n, c_m = _mlstm_pallas(u['c_q'], u['c_k'], u['c_v'], u['c_o'], u['c_z'], i_pre, f_pre,
                                              w_c_conv[l], g_c[l], c_conv0, c_c0, c_n0, c_m0,
                                              128 if long_seq else TPAD)
    qd = _rope(u['d_q'].reshape(b, t, H_D, DH_D), pos)
    kd = _rope(u['d_k'].reshape(b, t, H_D, DH_D), pos)
    vd = u['d_v'].reshape(b, t, H_D, DH_D)
    qi = _rope(u['d_qi'].reshape(b, t, H_I, D_I), pos)
    ki = _rope(u['d_ki'].reshape(b, t, 1, D_I), pos)[:, :, 0]
    wi = u['d_w'] * (H_I ** -0.5)
    if paged is None:
        od = _dsa_prompt_pallas(qd.reshape(b, t, W_BR), kd.reshape(b, t, W_BR), vd.reshape(b, t, W_BR),
                                qi.reshape(b, t, H_I * D_I), ki, u['d_w'], min(TOPK_MAX, t // 4))
    else:
        od = _dsa_sample_pallas(l, qd, kd, vd, qi, ki, u['d_w'], pd_ki, pd_k, pd_v, page_table)
    yd = od.reshape(b, t, W_BR) * jax.nn.silu(u['d_z'])
    ys = jnp.stack([ya, yb, yc, yd], 0).reshape(N_BRANCH, m, W_BR)
    new_x = _merge_pallas(ys, gate, w_branch[l], w_out[l], x.reshape(m, D_MODEL), ln_g[l], ln_b[l], min(m, 256))
    return new_x.reshape(b, t, D_MODEL), (kb, vb, kd, vd, ki, a_s, c_c, c_n, c_m, c_conv)


def kernel(x_prompt, x_sample, cache_b_k, cache_b_v, cache_d_k, cache_d_v, cache_d_ki,
           state_a_s, state_c_c, state_c_n, state_c_m, state_c_conv, page_table,
           w_in, w_a_gate, b_a_gate, g_a, lam_qk, g_b, b_c_if, w_c_conv, g_c,
           w_branch, w_out, ln_g, ln_b):
    bp, tp, _ = x_prompt.shape
    bd, td, _ = x_sample.shape
    past = page_table.shape[1] * PAGE_SIZE
    pos_p = jnp.arange(tp)
    pos_s = past + jnp.arange(td)
    dt = x_prompt.dtype
    bf16 = jnp.bfloat16
    w_in_p = (jnp.pad(w_in[:, :, :N_MAIN].astype(bf16), ((0, 0), (0, 0), (0, N_MAIN_PAD - N_MAIN))),
              w_in[:, :, N_MAIN:].astype(bf16))
    w_branch = w_branch.astype(bf16)
    w_out = w_out.astype(bf16)
    hp = x_prompt
    hs = x_sample
    new_p = []
    new_s = []
    for l in range(DEPTH):
        rec_p = (jnp.zeros((bp, H_A, DK_A, DV_A), dt), jnp.zeros((bp, H_C, DH_C, DH_C), dt),
                 jnp.zeros((bp, H_C, DH_C), dt), jnp.zeros((bp, H_C), dt),
                 jnp.zeros((bp, CONV_W - 1, 2 * W_BR), dt))
        hp, st_p = _sublayer(hp, pos_p, l, w_in_p, w_a_gate, b_a_gate, g_a, lam_qk, g_b, b_c_if,
                             w_c_conv, g_c, w_branch, w_out, ln_g, ln_b, rec_p, None)
        rec_s = (state_a_s[l], state_c_c[l], state_c_n[l], state_c_m[l], state_c_conv[l])
        paged = (page_table, cache_b_k, cache_b_v, cache_d_k, cache_d_v, cache_d_ki)
        hs, st_s = _sublayer(hs, pos_s, l, w_in_p, w_a_gate, b_a_gate, g_a, lam_qk, g_b, b_c_if,
                             w_c_conv, g_c, w_branch, w_out, ln_g, ln_b, rec_s, paged)
        new_p.append(st_p)
        new_s.append(st_s)
    outs_p = [jnp.stack([st[i] for st in new_p], 0) for i in range(10)]
    outs_s = [jnp.stack([st[i] for st in new_s], 0) for i in range(10)]
    return (hp, hs, *outs_p, *outs_s)
```

```python
import functools
import math
import jax
import jax.numpy as jnp
from jax import lax
import numpy as np
from jax.experimental import pallas as pl
from jax.experimental.pallas import tpu as pltpu

D_MODEL = 2048
DEPTH = 2
PAGE_SIZE = 128
N_BRANCH = 4
W_BR = D_MODEL // 4
H_A = 4
DK_A = W_BR // (2 * H_A)
DV_A = W_BR // H_A
GATE_RANK = 16
GATE_TAU = 16.0
GLA_CHUNK = 16
H_B = 4
DH_B = W_BR // (2 * H_B)
H_C = 4
DH_C = W_BR // H_C
CONV_W = 4
H_D = 4
DH_D = W_BR // H_D
H_I = 4
D_I = 64
TOPK_MAX = 256
ROPE_THETA = 10000.0
LN_EPS = 1e-5
NORM_EPS = 1e-6
DN_ALPHA = (2.0 * DEPTH) ** 0.25

IN_WIDTHS = (
    ('a_q', H_A * DK_A), ('a_k', H_A * DK_A), ('a_v', W_BR), ('a_g', GATE_RANK), ('a_z', W_BR),
    ('b_q', W_BR), ('b_k', W_BR), ('b_v', W_BR), ('b_z', W_BR),
    ('c_q', W_BR), ('c_k', W_BR), ('c_v', W_BR), ('c_i', H_C), ('c_f', H_C), ('c_o', W_BR), ('c_z', W_BR),
    ('d_q', W_BR), ('d_k', W_BR), ('d_v', W_BR), ('d_qi', H_I * D_I), ('d_ki', D_I), ('d_w', H_I), ('d_z', W_BR),
    ('gate', N_BRANCH * D_MODEL),
)
N_IN = sum(w for _, w in IN_WIDTHS)
N_MAIN = N_IN - N_BRANCH * D_MODEL
LANES = 128
N_MAIN_PAD = -(-N_MAIN // (4 * LANES)) * (4 * LANES)


def _mm_kernel(x_ref, w_ref, o_ref):
    o_ref[...] = jnp.dot(x_ref[...], w_ref[...], preferred_element_type=jnp.float32)


def _matmul(x, w, tm, tn):
    m, k = x.shape
    n = w.shape[1]
    return pl.pallas_call(
        _mm_kernel,
        grid=(n // tn, m // tm),
        in_specs=[pl.BlockSpec((tm, k), lambda j, i: (i, 0)),
                  pl.BlockSpec((k, tn), lambda j, i: (0, j))],
        out_specs=pl.BlockSpec((tm, tn), lambda j, i: (i, j)),
        out_shape=jax.ShapeDtypeStruct((m, n), jnp.float32),
        compiler_params=pltpu.CompilerParams(
            dimension_semantics=("arbitrary", "arbitrary"), vmem_limit_bytes=48 * 1024 * 1024),
        name="in_proj",
    )(x, w)


NEG_BIG = -1e30
INT_MIN = -2 ** 31
_NT = (((1,), (1,)), ((), ()))
VMEM_LIMIT = 48 * 1024 * 1024


def _split3(x):
    hi = x.astype(jnp.bfloat16)
    lo = (x - hi.astype(jnp.float32)).astype(jnp.bfloat16)
    return hi, lo


def _flash_consume(s, vh, carry):
    m, l, acc = carry
    m_new = jnp.maximum(m, jnp.max(s, axis=1, keepdims=True))
    alpha = jnp.exp(m - m_new)
    p = jnp.exp(s - m_new)
    l = alpha * l + jnp.sum(p, axis=1, keepdims=True)
    acc = alpha * acc + jnp.dot(p.astype(jnp.bfloat16), vh, preferred_element_type=jnp.float32)
    return m_new, l, acc


def _flash_consume_t(s_t, v_t, carry):
    m, l, acc = carry
    m_new = jnp.maximum(m, jnp.max(s_t, axis=0, keepdims=True))
    alpha = jnp.exp(m - m_new)
    p = jnp.exp(s_t - m_new)
    l = alpha * l + jnp.sum(p, axis=0, keepdims=True)
    acc = alpha * acc + jnp.dot(v_t, p.astype(jnp.bfloat16), preferred_element_type=jnp.float32)
    return m_new, l, acc


def _flash_init_t(tq, dv):
    return (jnp.full((1, tq), NEG_BIG, jnp.float32), jnp.zeros((1, tq), jnp.float32),
            jnp.zeros((dv, tq), jnp.float32))


def _chunked_t(v, kc):
    b, t, w = v.shape
    return v.astype(jnp.bfloat16).reshape(b, t // kc, kc, w).transpose(0, 1, 3, 2)


def _flash_init(tq, dv):
    return (jnp.full((tq, 1), NEG_BIG, jnp.float32), jnp.zeros((tq, 1), jnp.float32),
            jnp.zeros((tq, dv), jnp.float32))


SEARCH_ROWS = 128


def _kth_largest(key_ref, nc, n_sel):
    f32 = jnp.float32
    _, rows, kc = key_ref.shape
    step = min(SEARCH_ROWS, rows)
    assert rows % step == 0

    def search(r0):
        rs = pl.ds(r0, step)

        def count(pred, ref_val):
            def body(c, part):
                hit = jnp.where(pred(key_ref[c, rs, :], ref_val), 1.0, 0.0)
                for j in range(kc // LANES):
                    part = part + hit[:, j * LANES:(j + 1) * LANES]
                return part
            part = lax.fori_loop(0, nc, body, jnp.zeros((step, LANES), f32))
            return jnp.sum(part, axis=1, keepdims=True)

        def bit_body(it, tau):
            cand = tau ^ jnp.left_shift(jnp.int32(1), 31 - it)
            return jnp.where(count(lambda a, b: a >= b, cand) >= n_sel, cand, tau)
        tau = lax.fori_loop(0, 32, bit_body, jnp.full((step, 1), INT_MIN, jnp.int32))
        return tau, n_sel - count(lambda a, b: a > b, tau)
    parts = [search(r0) for r0 in range(0, rows, step)]
    return (jnp.concatenate([p[0] for p in parts], axis=0), jnp.concatenate([p[1] for p in parts], axis=0))


def _dsa_prompt_kernel(q_ref, k_ref, vt_ref, qi_ref, ki_ref, wi_ref, tri_ref, o_ref, key_ref, bias_ref,
                       *, tq, kc, n_sel, scale):
    f32 = jnp.float32
    i = pl.program_id(1)
    nc = (i * tq + tq + kc - 1) // kc
    qpos = i * tq + lax.broadcasted_iota(jnp.int32, (tq, 1), 0)
    lane = lax.broadcasted_iota(jnp.int32, (1, kc), 1)
    wi = wi_ref[0] * (H_I ** -0.5 * D_I ** -0.5)
    wis = [jnp.broadcast_to(wi[:, h:h + 1], (tq, kc)) for h in range(H_I)]
    d3 = 3 * D_I

    def score_body(c, carry):
        off = pl.multiple_of(c * kc, kc)
        kik = ki_ref[0, pl.ds(off, kc), :]
        acc = jnp.zeros((tq, kc), f32)
        for h in range(H_I):
            s = lax.dot_general(qi_ref[0, :, h * d3:(h + 1) * d3], kik, _NT, preferred_element_type=f32)
            acc = acc + wis[h] * jnp.maximum(s, 0.0)
        acc = jnp.where(off + lane <= qpos, acc, -jnp.inf)
        bits = pltpu.bitcast(acc, jnp.int32)
        bits = jnp.where(bits == INT_MIN, 0, bits)
        key_ref[c] = jnp.where(bits < 0, bits ^ 0x7FFFFFFF, bits)
        return carry
    lax.fori_loop(0, nc, score_body, 0)

    tau, need = _kth_largest(key_ref, nc, n_sel)

    def bias_body(c, seen):
        key = key_ref[c]
        tie = key == tau
        tie_f = jnp.where(tie, 1.0, 0.0)
        before = seen + jnp.dot(tie_f.astype(jnp.bfloat16), tri_ref[...], preferred_element_type=f32)
        sel = ((key > tau) | (tie & (before < need))) & (c * kc + lane <= qpos)
        bias_ref[c] = jnp.where(sel, 0.0, NEG_BIG).T
        return seen + jnp.sum(tie_f, axis=1, keepdims=True)
    lax.fori_loop(0, nc, bias_body, jnp.zeros((tq, 1), f32))

    heads = [slice(h * DH_D, (h + 1) * DH_D) for h in range(H_D)]
    qs = [(q_ref[0, :, cols] * scale).astype(jnp.bfloat16) for cols in heads]

    def logits(c):
        rows = pl.ds(pl.multiple_of(c * kc, kc), kc)
        return tuple(lax.dot_general(k_ref[0, rows, heads[h]], qs[h], _NT, preferred_element_type=f32)
                     for h in range(H_D))

    def att_body(c, carry):
        s_cur, states = carry
        s_next = logits(jnp.minimum(c + 1, nc - 1))
        bias = bias_ref[c]
        return s_next, tuple(_flash_consume_t(s_cur[h] + bias, vt_ref[0, c, heads[h], :], states[h])
                             for h in range(H_D))
    _, fin = lax.fori_loop(0, nc, att_body, (logits(0), tuple(_flash_init_t(tq, DH_D) for _ in range(H_D))))
    for h in range(H_D):
        o_ref[0, heads[h], :] = fin[h][2] / fin[h][1]


def _dsa_prompt_pallas(q, k, v, qi, ki, wi, n_sel, tq=256, kc=256):
    b, t, w = q.shape
    bf16 = jnp.bfloat16
    qh, ql = _split3(qi.reshape(b, t, H_I, D_I))
    kh, kl = _split3(ki)
    qi3 = jnp.concatenate([qh, qh, ql], -1).reshape(b, t, H_I * 3 * D_I)
    ki3 = jnp.concatenate([kh, kl, kh], -1)
    tri = jnp.asarray(np.triu(np.ones((kc, kc), np.float32), 1), bf16)
    kern = functools.partial(_dsa_prompt_kernel, tq=tq, kc=kc, n_sel=n_sel, scale=DH_D ** -0.5)
    blk = lambda bi, i: (bi, i, 0)
    full = lambda bi, i: (bi, 0, 0)
    return pl.pallas_call(
        kern,
        grid=(b, t // tq),
        in_specs=[pl.BlockSpec((1, tq, w), blk), pl.BlockSpec((1, t, w), full),
                  pl.BlockSpec((1, t // kc, w, kc), lambda bi, i: (bi, 0, 0, 0)),
                  pl.BlockSpec((1, tq, H_I * 3 * D_I), blk), pl.BlockSpec((1, t, 3 * D_I), full),
                  pl.BlockSpec((1, tq, H_I), blk), pl.BlockSpec((kc, kc), lambda bi, i: (0, 0))],
        out_specs=pl.BlockSpec((1, w, tq), lambda bi, i: (bi, 0, i)),
        out_shape=jax.ShapeDtypeStruct((b, w, t), jnp.float32),
        scratch_shapes=[pltpu.VMEM((t // kc, tq, kc), jnp.int32), pltpu.VMEM((t // kc, kc, tq), jnp.float32)],
        compiler_params=pltpu.CompilerParams(
            dimension_semantics=("arbitrary", "arbitrary"), vmem_limit_bytes=VMEM_LIMIT),
        name="dsa_prompt",
    )(q, k.astype(bf16), _chunked_t(v, kc), qi3, ki3, wi, tri).transpose(0, 2, 1)


def _diff_prompt_kernel(lam_ref, q_ref, k_ref, vt_ref, o_ref, *, tq, kc, scale):
    i = pl.program_id(1)
    nc = (i * tq + tq + kc - 1) // kc
    qpos = i * tq + lax.broadcasted_iota(jnp.int32, (1, tq), 1)
    kpos = lax.broadcasted_iota(jnp.int32, (kc, 1), 0)
    dcol = lax.broadcasted_iota(jnp.int32, (1, 2 * DH_B), 1)
    lam = lam_ref[0, 0]
    for pair in range(H_B // 2):
        heads = [slice(h * 2 * DH_B, (h + 1) * 2 * DH_B) for h in (2 * pair, 2 * pair + 1)]
        qms = [jnp.where((dcol >= m * DH_B) & (dcol < (m + 1) * DH_B), q_ref[0, :, cols] * scale, 0.0).astype(jnp.bfloat16)
               for cols in heads for m in range(2)]

        def logits(c, heads=heads, qms=qms):
            rows = pl.ds(pl.multiple_of(c * kc, kc), kc)
            return tuple(lax.dot_general(k_ref[0, rows, cols], qms[2 * j + m], _NT, preferred_element_type=jnp.float32)
                         for j, cols in enumerate(heads) for m in range(2))

        def att_body(c, carry, heads=heads, logits=logits):
            s_cur, states = carry
            s_next = logits(jnp.minimum(c + 1, nc - 1))
            bias = jnp.where(c * kc + kpos <= qpos, 0.0, NEG_BIG)
            out = []
            for j, cols in enumerate(heads):
                vh = vt_ref[0, c, cols, :]
                for m in range(2):
                    out.append(_flash_consume_t(s_cur[2 * j + m] + bias, vh, states[2 * j + m]))
            return s_next, tuple(out)
        _, fin = lax.fori_loop(0, nc, att_body, (logits(0), tuple(_flash_init_t(tq, 2 * DH_B) for _ in range(4))))
        for j, cols in enumerate(heads):
            o_ref[0, cols, :] = fin[2 * j][2] / fin[2 * j][1] - lam * (fin[2 * j + 1][2] / fin[2 * j + 1][1])


def _diff_prompt_pallas(q, k, v, lam, tq=256, kc=256):
    b, t, w = q.shape
    bf16 = jnp.bfloat16
    kern = functools.partial(_diff_prompt_kernel, tq=tq, kc=kc, scale=DH_B ** -0.5)
    blk = lambda bi, i: (bi, i, 0)
    full = lambda bi, i: (bi, 0, 0)
    return pl.pallas_call(
        kern,
        grid=(b, t // tq),
        in_specs=[pl.BlockSpec(memory_space=pltpu.SMEM),
                  pl.BlockSpec((1, tq, w), blk), pl.BlockSpec((1, t, w), full),
                  pl.BlockSpec((1, t // kc, w, kc), lambda bi, i: (bi, 0, 0, 0))],
        out_specs=pl.BlockSpec((1, w, tq), lambda bi, i: (bi, 0, i)),
        out_shape=jax.ShapeDtypeStruct((b, w, t), jnp.float32),
        compiler_params=pltpu.CompilerParams(
            dimension_semantics=("arbitrary", "arbitrary"), vmem_limit_bytes=VMEM_LIMIT),
        name="diff_prompt",
    )(lam.reshape(1, 1).astype(jnp.float32), q, k.astype(bf16), _chunked_t(v, kc)).transpose(0, 2, 1)


PPS = 16
TPAD = 8
NEG_INF_KEY = INT_MIN + 0x7FFFFF


def _float_order_key(x):
    bits = pltpu.bitcast(x, jnp.int32)
    bits = jnp.where(bits == INT_MIN, 0, bits)
    return jnp.where(bits < 0, bits ^ 0x7FFFFFFF, bits)


def _page_spec(l, r, rows, n_pages):
    def index(b, j, pt):
        return (l, pt[b * n_pages + j * PPS + r], 0, 0)
    return pl.BlockSpec((1, 1, rows, PAGE_SIZE), index)


def _rows_view(cache):
    return cache.reshape(cache.shape[0], cache.shape[1], cache.shape[2] * cache.shape[3], cache.shape[4])


def _online_softmax_step(state, s_list, pv):
    m_old, l, acc = state
    m_new = m_old
    for s in s_list:
        m_new = jnp.maximum(m_new, jnp.max(s, axis=1, keepdims=True))
    alpha = jnp.exp(m_old - m_new)
    p_list = [jnp.exp(s - m_new) for s in s_list]
    l = alpha * l
    for p in p_list:
        l = l + jnp.sum(p, axis=1, keepdims=True)
    return m_new, l, alpha * acc + pv(p_list)


def _online_softmax_update(m_ref, l_ref, acc_ref, s_list, pv):
    m_ref[...], l_ref[...], acc_ref[...] = _online_softmax_step((m_ref[...], l_ref[...], acc_ref[...]), s_list, pv)


def _sample_score_kernel(pt_ref, qh_ref, ql_ref, w_ref, kn_ref, *rest):
    ki_refs, (o_ref, on_ref) = rest[:PPS], rest[PPS:]
    f32 = jnp.float32
    qh, ql, w = qh_ref[0], ql_ref[0], w_ref[0]

    def scores(kt):
        kh, kl = _split3(kt)
        s = (jnp.dot(qh, kh, preferred_element_type=f32) + jnp.dot(qh, kl, preferred_element_type=f32)
             + jnp.dot(ql, kh, preferred_element_type=f32))
        r = w * jnp.maximum(s, 0.0)
        out = r[0:TPAD]
        for h in range(1, H_I):
            out = out + r[h * TPAD:(h + 1) * TPAD]
        return out
    for r in range(PPS):
        o_ref[0, :, r * PAGE_SIZE:(r + 1) * PAGE_SIZE] = scores(ki_refs[r][0, 0])

    @pl.when(pl.program_id(1) == 0)
    def _():
        t = lax.broadcasted_iota(jnp.int32, (TPAD, PAGE_SIZE), 0)
        j = lax.broadcasted_iota(jnp.int32, (TPAD, PAGE_SIZE), 1)
        on_ref[0] = jnp.where(j <= t, scores(kn_ref[0]), -jnp.inf)


def _sample_scores_pallas(l, qi, ki, wi, cache_ki, page_table):
    b, t = qi.shape[:2]
    n_pages = page_table.shape[1]
    qpad = jnp.zeros((b, H_I, TPAD, D_I), jnp.float32).at[:, :, :t].set(qi.transpose(0, 2, 1, 3))
    qh, ql = _split3(qpad.reshape(b, H_I * TPAD, D_I))
    wcol = jnp.zeros((b, H_I, TPAD), jnp.float32).at[:, :, :t].set(wi.transpose(0, 2, 1))
    wcol = (wcol * (H_I ** -0.5 * D_I ** -0.5)).reshape(b, H_I * TPAD, 1)
    knew = jnp.zeros((b, D_I, PAGE_SIZE), jnp.float32).at[:, :, :t].set(ki.transpose(0, 2, 1))
    row = lambda bi, j, pt: (bi, 0, 0)
    grid_spec = pltpu.PrefetchScalarGridSpec(
        num_scalar_prefetch=1, grid=(b, n_pages // PPS),
        in_specs=[pl.BlockSpec((1, H_I * TPAD, D_I), row), pl.BlockSpec((1, H_I * TPAD, D_I), row),
                  pl.BlockSpec((1, H_I * TPAD, 1), row), pl.BlockSpec((1, D_I, PAGE_SIZE), row)]
                 + [_page_spec(l, r, D_I, n_pages) for r in range(PPS)],
        out_specs=[pl.BlockSpec((1, TPAD, PPS * PAGE_SIZE), lambda bi, j, pt: (bi, 0, j)),
                   pl.BlockSpec((1, TPAD, PAGE_SIZE), row)])
    return pl.pallas_call(
        _sample_score_kernel, grid_spec=grid_spec,
        out_shape=[jax.ShapeDtypeStruct((b, TPAD, n_pages * PAGE_SIZE), jnp.float32),
                   jax.ShapeDtypeStruct((b, TPAD, PAGE_SIZE), jnp.float32)],
        compiler_params=pltpu.CompilerParams(dimension_semantics=("arbitrary", "arbitrary")),
        name="dsa_sample_scores",
    )(page_table.reshape(-1), qh, ql, wcol, knew, *([cache_ki.transpose(0, 1, 3, 2)] * PPS))


def _topk_bias_kernel(sc_ref, tri_ref, bias_ref, key_ref, *, n_sel):
    f32 = jnp.float32
    nc, rows, kc = sc_ref.shape

    def key_body(c, carry):
        key_ref[c] = _float_order_key(sc_ref[c])
        return carry
    lax.fori_loop(0, nc, key_body, 0)

    tau, need = _kth_largest(key_ref, nc, n_sel)

    def bias_body(c, seen):
        key = key_ref[c]
        tie = key == tau
        tie_f = jnp.where(tie, 1.0, 0.0)
        before = seen + jnp.dot(tie_f.astype(jnp.bfloat16), tri_ref[...], preferred_element_type=f32)
        sel = ((key > tau) | (tie & (before < need))) & (key > NEG_INF_KEY)
        bias_ref[c] = jnp.where(sel, 0.0, NEG_BIG)
        return seen + jnp.sum(tie_f, axis=1, keepdims=True)
    lax.fori_loop(0, nc, bias_body, jnp.zeros((rows, 1), f32))


def _topk_bias_pallas(scores, n_sel, kc=256):
    rows, n = scores.shape
    nc = n // kc
    tri = jnp.asarray(np.triu(np.ones((kc, kc), np.float32), 1), jnp.bfloat16)
    bias = pl.pallas_call(
        functools.partial(_topk_bias_kernel, n_sel=n_sel),
        out_shape=jax.ShapeDtypeStruct((nc, rows, kc), jnp.float32),
        scratch_shapes=[pltpu.VMEM((nc, rows, kc), jnp.int32)],
        compiler_params=pltpu.CompilerParams(vmem_limit_bytes=VMEM_LIMIT),
        name="topk_bias",
    )(scores.reshape(rows, nc, kc).transpose(1, 0, 2), tri)
    return bias.transpose(1, 0, 2).reshape(rows, n)


def _dsa_sample_attn_kernel(pt_ref, q_ref, kn_ref, vn_ref, bn_ref, bp_ref, *rest, scale):
    f32, bf16 = jnp.float32, jnp.bfloat16
    k_refs, v_refs = rest[:PPS], rest[PPS:2 * PPS]
    o_ref, m_ref, l_ref, acc_ref = rest[2 * PPS:]
    j = pl.program_id(1)
    wide = PAGE_SIZE * H_D

    @pl.when(j == 0)
    def _():
        m_ref[...] = jnp.full(m_ref.shape, NEG_BIG, f32)
        l_ref[...] = jnp.zeros(l_ref.shape, f32)
        acc_ref[...] = jnp.zeros(acc_ref.shape, f32)

    q = q_ref[0]
    s_list = [lax.dot_general(q, k_refs[r][0, 0].astype(bf16), _NT, preferred_element_type=f32) * scale
              + bp_ref[0, :, r * wide:(r + 1) * wide].astype(f32) for r in range(PPS)]

    def pv(p_list):
        out = jnp.dot(p_list[0].astype(bf16), v_refs[0][0, 0].astype(bf16), preferred_element_type=f32)
        for r in range(1, PPS):
            out = out + jnp.dot(p_list[r].astype(bf16), v_refs[r][0, 0].astype(bf16), preferred_element_type=f32)
        return out
    _online_softmax_update(m_ref, l_ref, acc_ref, s_list, pv)

    @pl.when(j == pl.num_programs(1) - 1)
    def _():
        s = lax.dot_general(q, kn_ref[0], _NT, preferred_element_type=f32) * scale + bn_ref[0]
        _online_softmax_update(m_ref, l_ref, acc_ref, [s],
                               lambda p: jnp.dot(p[0].astype(bf16), vn_ref[0], preferred_element_type=f32))
        o_ref[0] = acc_ref[...] / l_ref[...]


def _pad_tokens(x, fill=0.0):
    b, t = x.shape[:2]
    return jnp.full((b, TPAD) + x.shape[2:], fill, x.dtype).at[:, :t].set(x)


def _dsa_sample_pallas(l, q, k, v, qi, ki, wi_raw, cache_ki, cache_k, cache_v, page_table):
    b, t, h, dh = q.shape
    n_pages = page_table.shape[1]
    p_len = n_pages * PAGE_SIZE
    n_sel = min(TOPK_MAX, (p_len + t) // 4)
    bf16 = jnp.bfloat16
    s_past, s_new = _sample_scores_pallas(l, qi, ki, wi_raw, cache_ki, page_table)
    pad = jnp.full((b, t, PAGE_SIZE), -jnp.inf, jnp.float32)
    scores = jnp.concatenate([s_past[:, :t], s_new[:, :t], pad], -1).reshape(b * t, p_len + 2 * PAGE_SIZE)
    bias = _topk_bias_pallas(scores, n_sel).reshape(b, t, -1)
    rows = h * t
    same_head = jnp.arange(h)[:, None] == jnp.arange(h)[None, :]

    def per_head_bias(bz):
        full = jnp.where(same_head[None, :, None, None, :], bz[:, None, :, :, None], NEG_BIG)
        return full.reshape(b, rows, bz.shape[-1] * h)
    new_rows = TPAD * h
    bias_new = per_head_bias(jnp.full((b, t, TPAD), NEG_BIG, jnp.float32).at[:, :, :t].set(bias[:, :, p_len:p_len + t]))
    key_rows = lambda z: jnp.zeros((b, new_rows, dh), bf16).at[:, :t * h].set(z.reshape(b, t * h, dh).astype(bf16))
    row3 = lambda bi, j, pt: (bi, 0, 0)
    grid_spec = pltpu.PrefetchScalarGridSpec(
        num_scalar_prefetch=1, grid=(b, n_pages // PPS),
        in_specs=[pl.BlockSpec((1, rows, dh), row3), pl.BlockSpec((1, new_rows, dh), row3),
                  pl.BlockSpec((1, new_rows, dh), row3), pl.BlockSpec((1, rows, new_rows), row3),
                  pl.BlockSpec((1, rows, PPS * PAGE_SIZE * h), lambda bi, j, pt: (bi, 0, j))]
                 + [_page_spec(l, r, PAGE_SIZE * h, n_pages) for r in range(PPS)] * 2,
        out_specs=pl.BlockSpec((1, rows, dh), row3),
        scratch_shapes=[pltpu.VMEM((rows, 1), jnp.float32), pltpu.VMEM((rows, 1), jnp.float32),
                        pltpu.VMEM((rows, dh), jnp.float32)])
    res = pl.pallas_call(
        functools.partial(_dsa_sample_attn_kernel, scale=dh ** -0.5), grid_spec=grid_spec,
        out_shape=jax.ShapeDtypeStruct((b, rows, dh), jnp.float32),
        compiler_params=pltpu.CompilerParams(
            dimension_semantics=("arbitrary", "arbitrary"), vmem_limit_bytes=VMEM_LIMIT),
        name="dsa_sample_attn",
    )(page_table.reshape(-1), q.transpose(0, 2, 1, 3).reshape(b, rows, dh).astype(bf16), key_rows(k), key_rows(v),
      bias_new, per_head_bias(bias[:, :, :p_len].astype(bf16)),
      *([_rows_view(cache_k)] * PPS), *([_rows_view(cache_v)] * PPS))
    return res.reshape(b, h, t, dh).transpose(0, 2, 1, 3)


def _diff_sample_attn_kernel(pt_ref, q_ref, knt_ref, vn_ref, bn_ref, *rest, scale):
    f32, bf16 = jnp.float32, jnp.bfloat16
    kt_refs, v_refs = rest[:PPS], rest[PPS:2 * PPS]
    o_ref, m_ref, l_ref, acc_ref = rest[2 * PPS:]
    j = pl.program_id(1)
    rph = 2 * TPAD // 2

    @pl.when(j == 0)
    def _():
        m_ref[...] = jnp.full(m_ref.shape, NEG_BIG, f32)
        l_ref[...] = jnp.zeros(l_ref.shape, f32)
        acc_ref[...] = jnp.zeros(acc_ref.shape, f32)

    q = q_ref[0]
    s_list = [jnp.dot(q, kt_refs[r][0, 0].astype(bf16), preferred_element_type=f32) * scale for r in range(PPS)]

    def pv(p_list):
        outs = []
        for h in range(H_B):
            head = pl.ds(h, PAGE_SIZE, stride=H_B)
            rows = slice(h * rph, (h + 1) * rph)
            out = jnp.dot(p_list[0][rows].astype(bf16), v_refs[0][0, 0, head, :].astype(bf16), preferred_element_type=f32)
            for r in range(1, PPS):
                out = out + jnp.dot(p_list[r][rows].astype(bf16), v_refs[r][0, 0, head, :].astype(bf16),
                                    preferred_element_type=f32)
            outs.append(out)
        return jnp.concatenate(outs, axis=0)
    _online_softmax_update(m_ref, l_ref, acc_ref, s_list, pv)

    @pl.when(j == pl.num_programs(1) - 1)
    def _():
        s = jnp.dot(q, knt_ref[0], preferred_element_type=f32) * scale + bn_ref[0]

        def pv_new(p):
            return jnp.concatenate([jnp.dot(p[0][h * rph:(h + 1) * rph].astype(bf16), vn_ref[0, h],
                                            preferred_element_type=f32) for h in range(H_B)], axis=0)
        _online_softmax_update(m_ref, l_ref, acc_ref, [s], pv_new)
        o_ref[0] = acc_ref[...] / l_ref[...]


def _diff_sample_pallas(l, q, k, v, cache_k, cache_v, page_table, lam):
    b, t, h = q.shape[:3]
    n_pages = page_table.shape[1]
    w = h * 2 * DH_B
    dv = 2 * DH_B
    tp = TPAD // 2
    rows = 2 * h * tp
    bf16 = jnp.bfloat16
    blk = jnp.arange(w) // DH_B
    qpad = jnp.zeros((b, tp, w), jnp.float32).at[:, :t].set(q.reshape(b, t, w))
    qbd = jnp.where((blk[None, :] == jnp.arange(2 * h)[:, None])[None, :, None, :], qpad[:, None], 0.0)
    qbd = qbd.reshape(b, rows, w).astype(bf16)
    knt = jnp.zeros((b, w, TPAD), jnp.float32).at[:, :, :t].set(k.reshape(b, t, w).transpose(0, 2, 1)).astype(bf16)
    vn = _pad_tokens(v).transpose(0, 2, 1, 3).astype(bf16)
    causal = jnp.where(jnp.arange(TPAD)[None, :] <= jnp.arange(tp)[:, None], 0.0, NEG_BIG)
    causal = jnp.where(jnp.arange(TPAD)[None, :] < t, causal, NEG_BIG)
    bias_new = jnp.broadcast_to(jnp.tile(causal, (2 * h, 1))[None], (b, rows, TPAD))
    kt_view = cache_k.transpose(0, 1, 3, 4, 5, 2).reshape(cache_k.shape[0], cache_k.shape[1], w, PAGE_SIZE)
    row3 = lambda bi, j, pt: (bi, 0, 0)
    grid_spec = pltpu.PrefetchScalarGridSpec(
        num_scalar_prefetch=1, grid=(b, n_pages // PPS),
        in_specs=[pl.BlockSpec((1, rows, w), row3), pl.BlockSpec((1, w, TPAD), row3),
                  pl.BlockSpec((1, h, TPAD, dv), lambda bi, j, pt: (bi, 0, 0, 0)), pl.BlockSpec((1, rows, TPAD), row3)]
                 + [_page_spec(l, r, w, n_pages) for r in range(PPS)]
                 + [_page_spec(l, r, PAGE_SIZE * h, n_pages) for r in range(PPS)],
        out_specs=pl.BlockSpec((1, rows, dv), row3),
        scratch_shapes=[pltpu.VMEM((rows, 1), jnp.float32), pltpu.VMEM((rows, 1), jnp.float32),
                        pltpu.VMEM((rows, dv), jnp.float32)])
    res = pl.pallas_call(
        functools.partial(_diff_sample_attn_kernel, scale=DH_B ** -0.5), grid_spec=grid_spec,
        out_shape=jax.ShapeDtypeStruct((b, rows, dv), jnp.float32),
        compiler_params=pltpu.CompilerParams(
            dimension_semantics=("arbitrary", "arbitrary"), vmem_limit_bytes=VMEM_LIMIT),
        name="diff_sample_attn",
    )(page_table.reshape(-1), qbd, knt, vn, bias_new, *([kt_view] * PPS), *([_rows_view(cache_v)] * PPS))
    res = res.reshape(b, h, 2, tp, dv)[:, :, :, :t]
    return (res[:, :, 0] - lam * res[:, :, 1]).transpose(0, 2, 1, 3)


_TN = (((0,), (0,)), ((), ()))


def _silu(x):
    return x / (1.0 + jnp.exp(-x))


def _log_sigmoid(x):
    return jnp.minimum(x, 0.0) - jnp.log(1.0 + jnp.exp(-jnp.abs(x)))


def _head_rms_gate(o, g, z, d):
    outs = []
    for h in range(o.shape[1] // d):
        cols = slice(h * d, (h + 1) * d)
        oh = o[:, cols]
        outs.append(oh * lax.rsqrt(jnp.mean(oh * oh, axis=1, keepdims=True) + NORM_EPS))
    return jnp.concatenate(outs, axis=1) * g * _silu(z)


def _gla_kernel(q_ref, k_ref, v_ref, g_ref, wg_ref, bg_ref, z_ref, gn_ref, tril_ref, s0_ref,
                y_ref, sout_ref, st_ref, la_ref, o_ref, *, tb, t_valid):
    f32, bf16 = jnp.float32, jnp.bfloat16
    c = GLA_CHUNK

    @pl.when(pl.program_id(1) == 0)
    def _():
        st_ref[...] = s0_ref[0]

    gl = jnp.dot(g_ref[0].astype(bf16), wg_ref[...].astype(bf16), preferred_element_type=f32) + bg_ref[...]
    la = _log_sigmoid(gl) / GATE_TAU
    if t_valid < tb:
        la = jnp.where(lax.broadcasted_iota(jnp.int32, (tb, 1), 0) < t_valid, la, 0.0)
    la_ref[...] = la
    trow = lax.broadcasted_iota(jnp.int32, (c, 1), 0)
    lane_id = lax.broadcasted_iota(jnp.int32, (1, LANES), 1)

    def chunk(ci, carry):
        rows = pl.ds(pl.multiple_of(ci * c, c), c)
        cb = jnp.dot(tril_ref[...], la_ref[rows, :], precision=lax.Precision.HIGHEST, preferred_element_type=f32)
        c_last = cb[c - 1:c, :]
        qc = q_ref[0, rows, :] * (DK_A ** -0.5)
        kc = k_ref[0, rows, :]
        vc = v_ref[0, rows, :]
        qe = (qc * jnp.exp(cb)).astype(bf16)
        kdec = (kc * jnp.exp(c_last - cb)).astype(bf16)
        dec = jnp.exp(c_last)
        s_ts = [st_ref[h] for h in range(H_A)]
        intra = [[] for _ in range(H_A)]
        for pair in range(H_A // 2):
            slab = slice(pair * LANES, (pair + 1) * LANES)
            q_p, k_p, cb_p = qc[:, slab], kc[:, slab], cb[:, slab]
            for s in range(c):
                x = q_p * k_p[s:s + 1, :] * jnp.exp(cb_p - cb_p[s:s + 1, :])
                for half in range(2):
                    h = 2 * pair + half
                    col = jnp.sum(jnp.where((lane_id >= half * DK_A) & (lane_id < (half + 1) * DK_A), x, 0.0),
                                  axis=1, keepdims=True)
                    intra[h].append(jnp.where(trow >= s, col, 0.0) * vc[s:s + 1, h * DV_A:(h + 1) * DV_A])
        outs = []
        for h in range(H_A):
            kcols = slice(h * DK_A, (h + 1) * DK_A)
            parts = intra[h]
            while len(parts) > 1:
                parts = [parts[i] + parts[i + 1] for i in range(0, len(parts), 2)]
            outs.append(parts[0] + lax.dot_general(qe[:, kcols], s_ts[h].astype(bf16), _NT, preferred_element_type=f32))
            s_ts[h] = s_ts[h] * dec[:, kcols] + lax.dot_general(
                vc[:, h * DV_A:(h + 1) * DV_A].astype(bf16), kdec[:, kcols], _TN, preferred_element_type=f32)
        o_ref[rows, :] = jnp.concatenate(outs, axis=1)
        for h in range(H_A):
            st_ref[h] = s_ts[h]
        return carry
    lax.fori_loop(0, tb // c, chunk, 0, unroll=2 if tb // c > 1 else 1)
    y_ref[0] = _head_rms_gate(o_ref[...], gn_ref[...], z_ref[0], DV_A)

    @pl.when(pl.program_id(1) == pl.num_programs(1) - 1)
    def _():
        sout_ref[0] = st_ref[...]


def _gla_pallas(q, k, v, g, z, w_gate, b_gate, g_norm, s0, tb):
    b, t, _ = q.shape
    tp = -(-t // tb) * tb
    pad = lambda x: x if tp == t else jnp.zeros((b, tp, x.shape[2]), x.dtype).at[:, :t].set(x)
    tril = jnp.asarray(np.tril(np.ones((GLA_CHUNK, GLA_CHUNK), np.float32)))
    blk = lambda w: pl.BlockSpec((1, tb, w), lambda bi, j: (bi, j, 0))
    const = lambda shape: pl.BlockSpec(shape, lambda bi, j: (0,) * len(shape))
    st_spec = pl.BlockSpec((1, H_A, DV_A, DK_A), lambda bi, j: (bi, 0, 0, 0))
    y, s_t = pl.pallas_call(
        functools.partial(_gla_kernel, tb=tb, t_valid=min(t, tb)),
        grid=(b, tp // tb),
        in_specs=[blk(H_A * DK_A), blk(H_A * DK_A), blk(W_BR), blk(GATE_RANK),
                  const((GATE_RANK, H_A * DK_A)), const((1, H_A * DK_A)), blk(W_BR), const((1, W_BR)),
                  const((GLA_CHUNK, GLA_CHUNK)), st_spec],
        out_specs=[blk(W_BR), st_spec],
        out_shape=[jax.ShapeDtypeStruct((b, tp, W_BR), jnp.float32),
                   jax.ShapeDtypeStruct((b, H_A, DV_A, DK_A), jnp.float32)],
        scratch_shapes=[pltpu.VMEM((H_A, DV_A, DK_A), jnp.float32), pltpu.VMEM((tb, H_A * DK_A), jnp.float32),
                        pltpu.VMEM((tb, W_BR), jnp.float32)],
        compiler_params=pltpu.CompilerParams(
            dimension_semantics=("arbitrary", "arbitrary"), vmem_limit_bytes=VMEM_LIMIT),
        name="gla",
    )(pad(q), pad(k), pad(v), pad(g), w_gate, b_gate.reshape(1, -1), pad(z), g_norm.reshape(1, -1), tril,
      s0.swapaxes(2, 3))
    return y[:, :t], s_t.swapaxes(2, 3)


CONV_PAD = 8


def _mlstm_kernel(cq_ref, ck_ref, cv_ref, co_ref, cz_ref, icol_ref, fcol_ref, irow_ref, frow_ref,
                  wq_ref, wk_ref, gn_ref, tril_ref, conv0_ref, c0_ref, n0_ref, m0_ref,
                  y_ref, conv_ref, c_ref, n_ref, m_ref, xq_ref, xk_ref, cm_ref, nv_ref, mv_ref, *, ch, t_valid):
    f32, bf16 = jnp.float32, jnp.bfloat16
    hi = lax.Precision.HIGHEST
    first = CONV_PAD - (CONV_W - 1)

    @pl.when(pl.program_id(1) == 0)
    def _():
        xq_ref[0:CONV_PAD, :] = conv0_ref[0, :, 0:W_BR]
        xk_ref[0:CONV_PAD, :] = conv0_ref[0, :, W_BR:2 * W_BR]
        cm_ref[...] = c0_ref[0]
        nv_ref[...] = n0_ref[0]
        mv_ref[...] = m0_ref[0]

    def conv(x_ref, new_ref, w_ref, cols):
        x_ref[CONV_PAD:CONV_PAD + ch, :] = new_ref[0]
        out = x_ref[first:first + ch, :] * w_ref[0:1, :]
        for j in range(1, CONV_W):
            out = out + x_ref[first + j:first + j + ch, :] * w_ref[j:j + 1, :]
        conv_ref[0, :, cols] = x_ref[first + t_valid:first + t_valid + CONV_W - 1, :]
        tail = x_ref[ch:ch + CONV_PAD, :]
        x_ref[0:CONV_PAD, :] = tail
        return _silu(out)
    qc = conv(xq_ref, cq_ref, wq_ref, slice(0, W_BR))
    kc = conv(xk_ref, ck_ref, wk_ref, slice(W_BR, 2 * W_BR)) * (DH_C ** -0.5)

    tril = tril_ref[...]
    fcum_c = jnp.dot(tril, _log_sigmoid(fcol_ref[0]), precision=hi, preferred_element_type=f32)
    fcum_r = lax.dot_general(_log_sigmoid(frow_ref[0]), tril, _NT, precision=hi, preferred_element_type=f32)
    i_c, i_r = icol_ref[0], irow_ref[0]
    causal = lax.broadcasted_iota(jnp.int32, (ch, ch), 1) <= lax.broadcasted_iota(jnp.int32, (ch, ch), 0)
    outs = []
    for h in range(H_C):
        cols = slice(h * DH_C, (h + 1) * DH_C)
        q_h, k_h, v_h = qc[:, cols], kc[:, cols], cv_ref[0, :, cols]
        cm, nv, m_prev = cm_ref[h], nv_ref[h:h + 1, :], mv_ref[h:h + 1, 0:1]
        fc, fr, ic, ir = fcum_c[:, h:h + 1], fcum_r[h:h + 1, :], i_c[:, h:h + 1], i_r[h:h + 1, :]
        dmat = jnp.where(causal, fc - fr + ir, NEG_BIG)
        inter = fc + m_prev
        m_t = jnp.maximum(inter, jnp.max(dmat, axis=1, keepdims=True))
        w_inter = jnp.exp(inter - m_t)
        qk = lax.dot_general(q_h.astype(bf16), k_h.astype(bf16), _NT, preferred_element_type=f32) * jnp.exp(dmat - m_t)
        num = (w_inter * jnp.dot(q_h.astype(bf16), cm.astype(bf16), preferred_element_type=f32)
               + jnp.dot(qk.astype(bf16), v_h.astype(bf16), preferred_element_type=f32))
        den = w_inter * jnp.sum(q_h * nv, axis=1, keepdims=True) + jnp.sum(qk, axis=1, keepdims=True)
        hv = num / jnp.maximum(jnp.abs(den), jnp.exp(-m_t))
        outs.append(hv / (1.0 + jnp.exp(-co_ref[0, :, cols])))
        f_last = fc[ch - 1:ch, :]
        m_new = jnp.maximum(f_last + m_prev, jnp.max(f_last - fr + ir, axis=1, keepdims=True))
        a = jnp.exp(f_last + m_prev - m_new)
        kw = k_h * jnp.exp(f_last - fc + ic - m_new)
        cm_ref[h] = a * cm + lax.dot_general(kw.astype(bf16), v_h.astype(bf16), _TN, preferred_element_type=f32)
        nv_ref[h:h + 1, :] = a * nv + jnp.sum(kw, axis=0, keepdims=True)
        mv_ref[h:h + 1, :] = jnp.broadcast_to(m_new, (1, LANES))
    y_ref[0] = _head_rms_gate(jnp.concatenate(outs, axis=1), gn_ref[...], cz_ref[0], DH_C)

    @pl.when(pl.program_id(1) == pl.num_programs(1) - 1)
    def _():
        c_ref[0] = cm_ref[...]
        n_ref[0] = nv_ref[...]
        m_ref[0] = mv_ref[...]


def _mlstm_pallas(cq, ck, cv, co, cz, i_pre, f_pre, w_conv, g_norm, conv0, c0, n0, m0, ch):
    b, t, _ = cq.shape
    tp = -(-t // ch) * ch
    f32 = jnp.float32
    pad = lambda x, fill=0.0: x if tp == t else jnp.full((b, tp, x.shape[2]), fill, x.dtype).at[:, :t].set(x)
    i_pre, f_pre = pad(i_pre, NEG_BIG), pad(f_pre, -NEG_BIG)
    rows8 = lambda x: jnp.zeros((b, 8, x.shape[2]), f32).at[:, :x.shape[1]].set(x)
    tril = jnp.asarray(np.tril(np.ones((ch, ch), np.float32)))
    conv_hist = jnp.zeros((b, CONV_PAD, 2 * W_BR), f32).at[:, CONV_PAD - (CONV_W - 1):].set(conv0)
    assert tp == t or tp == ch
    blk = lambda w: pl.BlockSpec((1, ch, w), lambda bi, j: (bi, j, 0))
    rowblk = pl.BlockSpec((1, 8, ch), lambda bi, j: (bi, 0, j))
    const = lambda shape: pl.BlockSpec(shape, lambda bi, j: (0,) * len(shape))
    per_b = lambda shape: pl.BlockSpec((1,) + shape, lambda bi, j: (bi,) + (0,) * len(shape))
    y, conv_s, c_f, n_f, m_f = pl.pallas_call(
        functools.partial(_mlstm_kernel, ch=ch, t_valid=t - (tp - ch)),
        grid=(b, tp // ch),
        in_specs=[blk(W_BR)] * 5 + [blk(H_C), blk(H_C), rowblk, rowblk,
                  const((CONV_W, W_BR)), const((CONV_W, W_BR)), const((1, W_BR)), const((ch, ch)),
                  per_b((CONV_PAD, 2 * W_BR)), per_b((H_C, DH_C, DH_C)), per_b((8, DH_C)), per_b((8, LANES))],
        out_specs=[blk(W_BR), per_b((CONV_W - 1, 2 * W_BR)), per_b((H_C, DH_C, DH_C)), per_b((8, DH_C)),
                   per_b((8, LANES))],
        out_shape=[jax.ShapeDtypeStruct((b, tp, W_BR), f32), jax.ShapeDtypeStruct((b, CONV_W - 1, 2 * W_BR), f32),
                   jax.ShapeDtypeStruct((b, H_C, DH_C, DH_C), f32), jax.ShapeDtypeStruct((b, 8, DH_C), f32),
                   jax.ShapeDtypeStruct((b, 8, LANES), f32)],
        scratch_shapes=[pltpu.VMEM((CONV_PAD + ch, W_BR), f32), pltpu.VMEM((CONV_PAD + ch, W_BR), f32),
                        pltpu.VMEM((H_C, DH_C, DH_C), f32), pltpu.VMEM((8, DH_C), f32), pltpu.VMEM((8, LANES), f32)],
        compiler_params=pltpu.CompilerParams(
            dimension_semantics=("arbitrary", "arbitrary"), vmem_limit_bytes=VMEM_LIMIT),
        name="mlstm",
    )(pad(cq), pad(ck), pad(cv), pad(co), pad(cz), i_pre, f_pre,
      rows8(i_pre.transpose(0, 2, 1)), rows8(f_pre.transpose(0, 2, 1)),
      w_conv[:, :W_BR], w_conv[:, W_BR:], g_norm.reshape(1, -1), tril, conv_hist, c0, rows8(n0),
      rows8(jnp.broadcast_to(m0[:, :, None], (b, H_C, LANES))))
    return y[:, :t], conv_s, c_f, n_f[:, :H_C], m_f[:, :H_C, 0]


def _merge_kernel(ys_ref, gate_ref, wb_ref, wo_ref, x_ref, g_ref, b_ref, o_ref, acc_ref):
    f32, bf16 = jnp.float32, jnp.bfloat16
    n = pl.program_id(1)
    proj = jnp.dot(ys_ref[0].astype(bf16), wb_ref[0], preferred_element_type=f32)
    term = proj / (1.0 + jnp.exp(-gate_ref[...]))

    @pl.when(n == 0)
    def _():
        acc_ref[...] = term

    @pl.when(n > 0)
    def _():
        acc_ref[...] += term

    @pl.when(n == N_BRANCH - 1)
    def _():
        out = jnp.dot(acc_ref[...].astype(bf16), wo_ref[...], preferred_element_type=f32)
        xf = DN_ALPHA * x_ref[...] + out
        xc = xf - jnp.mean(xf, axis=1, keepdims=True)
        var = jnp.mean(xc * xc, axis=1, keepdims=True)
        o_ref[...] = xc * lax.rsqrt(var + LN_EPS) * g_ref[...] + b_ref[...]


def _merge_pallas(ys, gate, w_branch, w_out, x, ln_g, ln_b, tm):
    m = x.shape[0]
    return pl.pallas_call(
        _merge_kernel,
        grid=(m // tm, N_BRANCH),
        in_specs=[pl.BlockSpec((1, tm, W_BR), lambda i, n: (n, i, 0)),
                  pl.BlockSpec((tm, D_MODEL), lambda i, n: (i, n)),
                  pl.BlockSpec((1, W_BR, D_MODEL), lambda i, n: (n, 0, 0)),
                  pl.BlockSpec((D_MODEL, D_MODEL), lambda i, n: (0, 0)),
                  pl.BlockSpec((tm, D_MODEL), lambda i, n: (i, 0)),
                  pl.BlockSpec((1, D_MODEL), lambda i, n: (0, 0)),
                  pl.BlockSpec((1, D_MODEL), lambda i, n: (0, 0))],
        out_specs=pl.BlockSpec((tm, D_MODEL), lambda i, n: (i, 0)),
        out_shape=jax.ShapeDtypeStruct((m, D_MODEL), jnp.float32),
        scratch_shapes=[pltpu.VMEM((tm, D_MODEL), jnp.float32)],
        compiler_params=pltpu.CompilerParams(
            dimension_semantics=("arbitrary", "arbitrary"), vmem_limit_bytes=VMEM_LIMIT),
        name="merge",
    )(ys, gate, w_branch, w_out, x, ln_g.reshape(1, -1), ln_b.reshape(1, -1))


def _split_in(u):
    parts = {}
    off = 0
    for name, width in IN_WIDTHS[:-1]:
        parts[name] = u[..., off:off + width]
        off += width
    return parts


def _rope(x, pos):
    half = x.shape[-1] // 2
    inv = ROPE_THETA ** (-jnp.arange(half, dtype=jnp.float32) / half)
    ang = pos.astype(jnp.float32)[:, None] * inv[None, :]
    cos = jnp.cos(ang)[:, None, :]
    sin = jnp.sin(ang)[:, None, :]
    x1 = x[..., :half].astype(jnp.float32)
    x2 = x[..., half:].astype(jnp.float32)
    return jnp.concatenate([x1 * cos - x2 * sin, x1 * sin + x2 * cos], -1).astype(x.dtype)


def _head_rms(h, g):
    hf = h.astype(jnp.float32)
    hf = hf * lax.rsqrt(jnp.mean(hf * hf, -1, keepdims=True) + NORM_EPS)
    return (hf.reshape(h.shape[:-2] + (-1,)) * g).astype(h.dtype)


def _sublayer(x, pos, l, w_in_p, w_a_gate, b_a_gate, g_a, lam_qk, g_b, b_c_if, w_c_conv, g_c,
              w_branch, w_out, ln_g, ln_b, rec, paged):
    b, t, _ = x.shape
    f32 = jnp.float32
    a_s0, c_c0, c_n0, c_m0, c_conv0 = rec
    m = b * t
    w_main, w_gate = w_in_p
    xb = x.reshape(m, D_MODEL).astype(jnp.bfloat16)
    u = _split_in(_matmul(xb, w_main[l], min(m, 512), N_MAIN_PAD // 4).reshape(b, t, N_MAIN_PAD))
    gate = _matmul(xb, w_gate[l], min(m, 512), D_MODEL)
    long_seq = t >= 512
    ya, a_s = _gla_pallas(u['a_q'], u['a_k'], u['a_v'], u['a_g'], u['a_z'], w_a_gate[l], b_a_gate[l], g_a[l], a_s0,
                          512 if long_seq else GLA_CHUNK)
    qb = _rope(u['b_q'].reshape(b, t, 2 * H_B, DH_B), pos).reshape(b, t, H_B, 2, DH_B)
    kb = _rope(u['b_k'].reshape(b, t, 2 * H_B, DH_B), pos).reshape(b, t, H_B, 2, DH_B)
    vb = u['b_v'].reshape(b, t, H_B, 2 * DH_B)
    lam_init = 0.8 - 0.6 * math.exp(-0.3 * l)
    lq = lam_qk[l].astype(f32)
    lam = jnp.exp(jnp.sum(lq[0] * lq[1])) - jnp.exp(jnp.sum(lq[2] * lq[3])) + lam_init
    if paged is None:
        ob = _diff_prompt_pallas(qb.reshape(b, t, W_BR), kb.reshape(b, t, W_BR), vb.reshape(b, t, W_BR),
                                 lam).reshape(b, t, H_B, 2 * DH_B)
    else:
        page_table, pb_k, pb_v, pd_k, pd_v, pd_ki = paged
        ob = _diff_sample_pallas(l, qb, kb, vb, pb_k, pb_v, page_table, lam)
    yb = _head_rms(ob, g_b[l]) * (1.0 - lam_init) * jax.nn.silu(u['b_z'])
    i_pre = u['c_i'].astype(f32) + b_c_if[l, 0]
    f_pre = u['c_f'].astype(f32) + b_c_if[l, 1]
    yc, c_conv, c_c, c_n, c_m = _mlstm_pallas(u['c_q'], u['c_k'], u['c_v'], u['c_o'], u['c_z'], i_pre, f_pre,
                                              w_c_conv[l], g_c[l], c_conv0, c_c0, c_n0, c_m0,
                                              128 if long_seq else TPAD)
    qd = _rope(u['d_q'].reshape(b, t, H_D, DH_D), pos)
    kd = _rope(u['d_k'].reshape(b, t, H_D, DH_D), pos)
    vd = u['d_v'].reshape(b, t, H_D, DH_D)
    qi = _rope(u['d_qi'].reshape(b, t, H_I, D_I), pos)
    ki = _rope(u['d_ki'].reshape(b, t, 1, D_I), pos)[:, :, 0]
    wi = u['d_w'] * (H_I ** -0.5)
    if paged is None:
        od = _dsa_prompt_pallas(qd.reshape(b, t, W_BR), kd.reshape(b, t, W_BR), vd.reshape(b, t, W_BR),
                                qi.reshape(b, t, H_I * D_I), ki, u['d_w'], min(TOPK_MAX, t // 4))
    else:
        od = _dsa_sample_pallas(l, qd, kd, vd, qi, ki, u['d_w'], pd_ki, pd_k, pd_v, page_table)
    yd = od.reshape(b, t, W_BR) * jax.nn.silu(u['d_z'])
    ys = jnp.stack([ya, yb, yc, yd], 0).reshape(N_BRANCH, m, W_BR)
    new_x = _merge_pallas(ys, gate, w_branch[l], w_out[l], x.reshape(m, D_MODEL), ln_g[l], ln_b[l], min(m, 256))
    return new_x.reshape(b, t, D_MODEL), (kb, vb, kd, vd, ki, a_s, c_c, c_n, c_m, c_conv)


def kernel(x_prompt, x_sample, cache_b_k, cache_b_v, cache_d_k, cache_d_v, cache_d_ki,
           state_a_s, state_c_c, state_c_n, state_c_m, state_c_conv, page_table,
           w_in, w_a_gate, b_a_gate, g_a, lam_qk, g_b, b_c_if, w_c_conv, g_c,
           w_branch, w_out, ln_g, ln_b):
    bp, tp, _ = x_prompt.shape
    bd, td, _ = x_sample.shape
    past = page_table.shape[1] * PAGE_SIZE
    pos_p = jnp.arange(tp)
    pos_s = past + jnp.arange(td)
    dt = x_prompt.dtype
    bf16 = jnp.bfloat16
    w_in_p = (jnp.pad(w_in[:, :, :N_MAIN].astype(bf16), ((0, 0), (0, 0), (0, N_MAIN_PAD - N_MAIN))),
              w_in[:, :, N_MAIN:].astype(bf16))
    w_branch = w_branch.astype(bf16)
    w_out = w_out.astype(bf16)
    hp = x_prompt
    hs = x_sample
    new_p = []
    new_s = []
    for l in range(DEPTH):
        rec_p = (jnp.zeros((bp, H_A, DK_A, DV_A), dt), jnp.zeros((bp, H_C, DH_C, DH_C), dt),
                 jnp.zeros((bp, H_C, DH_C), dt), jnp.zeros((bp, H_C), dt),
                 jnp.zeros((bp, CONV_W - 1, 2 * W_BR), dt))
        hp, st_p = _sublayer(hp, pos_p, l, w_in_p, w_a_gate, b_a_gate, g_a, lam_qk, g_b, b_c_if,
                             w_c_conv, g_c, w_branch, w_out, ln_g, ln_b, rec_p, None)
        rec_s = (state_a_s[l], state_c_c[l], state_c_n[l], state_c_m[l], state_c_conv[l])
        paged = (page_table, cache_b_k, cache_b_v, cache_d_k, cache_d_v, cache_d_ki)
        hs, st_s = _sublayer(hs, pos_s, l, w_in_p, w_a_gate, b_a_gate, g_a, lam_qk, g_b, b_c_if,
                             w_c_conv, g_c, w_branch, w_out, ln_g, ln_b, rec_s, paged)
        new_p.append(st_p)
        new_s.append(st_s)
    outs_p = [jnp.stack([st[i] for st in new_p], 0) for i in range(10)]
    outs_s = [jnp.stack([st[i] for st in new_s], 0) for i in range(10)]
    return (hp, hs, *outs_p, *outs_s)
```
